```python
import jax, jax.numpy as jnp
from jax import lax
import numpy as np

D_MODEL = 2048
BATCH = 1
SEQ = 16384
DEPTH = 1
DEC_BATCH = 32
DEC_SEQ = 32
PAST_LEN = 1024

CHUNK = 64
Q_BLOCK = 128
A_HEADS = 8
A_HEAD_DIM = 128
A_WIDTH = A_HEADS * A_HEAD_DIM
IDX_HEADS = 8
IDX_DIM = 64
TOPK_MAX = 256
ATTN_SCALE = A_HEAD_DIM ** -0.5
IDX_SCALE = IDX_DIM ** -0.5
IDX_W_SCALE = IDX_HEADS ** -0.5
B_HEADS = 8
B_HEAD_DIM = 128
B_WIDTH = B_HEADS * B_HEAD_DIM
RET_K_SCALE = B_HEAD_DIM ** -0.5
MIX_WIDTH = A_WIDTH + B_WIDTH
OFF_AK = A_WIDTH
OFF_AV = 2 * A_WIDTH
OFF_IQ = 3 * A_WIDTH
OFF_IK = OFF_IQ + IDX_HEADS * IDX_DIM
OFF_IW = OFF_IK + IDX_DIM
OFF_RQ = OFF_IW + IDX_HEADS
OFF_RK = OFF_RQ + B_WIDTH
OFF_RV = OFF_RK + B_WIDTH
OFF_RG = OFF_RV + B_WIDTH
IN_WIDTH = OFF_RG + B_WIDTH
IN_SPLIT_POINTS = (OFF_AK, OFF_AV, OFF_IQ, OFF_IK, OFF_IW, OFF_RQ, OFF_RK, OFF_RV, OFF_RG)
N_GROUPS = 4
EXPERTS_PER_GROUP = 4
N_EXPERTS = N_GROUPS * EXPERTS_PER_GROUP
D_EXPERT = 512
EPS = 1e-6

kernel_name = 'hybrid_dsa_retention_hmoe_stream_step'


def rms_norm(x, g=None):
    xf = x.astype(jnp.float32)
    y = xf * lax.rsqrt(jnp.mean(xf * xf, axis=-1, keepdims=True) + EPS)
    if g is not None:
        y = y * g.astype(jnp.float32)
    return y.astype(x.dtype)


def alibi_slopes():
    return 2.0 ** (-8.0 * jnp.arange(1, A_HEADS + 1, dtype=jnp.float32) / A_HEADS)


def retention_log_decay():
    return jnp.log1p(-(2.0 ** (-5.0 - jnp.arange(B_HEADS, dtype=jnp.float32))))


def sparse_attend_seq(q, qi, wi, tq, k_all, v_all, ki_all, topk, slopes):
    L = k_all.shape[0]
    allowed = (jnp.arange(L, dtype=jnp.int32)[None, :] // CHUNK) <= (tq[:, None] // CHUNK)
    rel = jax.nn.relu(jnp.einsum('qhd,ld->qhl', qi, ki_all).astype(jnp.float32) * IDX_SCALE)
    iscore = jnp.einsum('qhl,qh->ql', rel, wi.astype(jnp.float32) * IDX_W_SCALE)
    iscore = jnp.where(allowed, iscore, -jnp.inf)
    _, idx = lax.top_k(iscore, topk)
    k_sel = k_all[idx]
    v_sel = v_all[idx]
    valid = (idx // CHUNK) <= (tq[:, None] // CHUNK)
    dist = jnp.abs(tq[:, None] - idx).astype(jnp.float32)
    logits = (jnp.einsum('qhd,qkhd->qhk', q, k_sel).astype(jnp.float32) * ATTN_SCALE
              - slopes[None, :, None] * dist[:, None, :])
    logits = jnp.where(valid[:, None, :], logits, -jnp.inf)
    p = jax.nn.softmax(logits, axis=-1).astype(v_all.dtype)
    return jnp.einsum('qhk,qkhd->qhd', p, v_sel)


def attention_prompt(q, k, v, qi, ki, wi, slopes):
    T = q.shape[1]
    topk = min(TOPK_MAX, T // 4)
    nb = T // Q_BLOCK
    pos = jnp.arange(T, dtype=jnp.int32)

    def one_seq(args):
        qs, ks, vs, qis, kis, wis = args

        def one_block(bargs):
            qb, qib, wib, tqb = bargs
            return sparse_attend_seq(qb, qib, wib, tqb, ks, vs, kis, topk, slopes)

        blocks = lambda a: a.reshape((nb, Q_BLOCK) + a.shape[1:])
        out = lax.map(one_block, (blocks(qs), blocks(qis), blocks(wis), pos.reshape(nb, Q_BLOCK)))
        return out.reshape(qs.shape)

    return lax.map(one_seq, (q, k, v, qi, ki, wi))


def attention_sample(q, k, v, qi, ki, wi, ck, cv, cki, slopes):
    P = ck.shape[1]
    Tn = q.shape[1]
    topk = min(TOPK_MAX, (P + Tn) // 4)
    tq = P + jnp.arange(Tn, dtype=jnp.int32)
    k_all = jnp.concatenate([ck, k], axis=1)
    v_all = jnp.concatenate([cv, v], axis=1)
    ki_all = jnp.concatenate([cki, ki], axis=1)

    def one_seq(args):
        qs, qis, wis, ka, va, kia = args
        return sparse_attend_seq(qs, qis, wis, tq, ka, va, kia, topk, slopes)

    return lax.map(one_seq, (q, qi, wi, k_all, v_all, ki_all))


def retention_chunk(q, k, v, state, log_gamma):
    dt = q.dtype
    C = q.shape[1]
    pos = jnp.arange(C, dtype=jnp.float32)
    diff = pos[:, None] - pos[None, :]
    decay = jnp.where(diff[None] >= 0, jnp.exp(log_gamma[:, None, None] * jnp.maximum(diff, 0.0)[None]), 0.0).astype(dt)
    scores = jnp.einsum('bihd,bjhd->bhij', q, k) * decay[None]
    o = jnp.einsum('bhij,bjhv->bihv', scores, v)
    cross = jnp.exp(log_gamma[None, :] * (pos[:, None] + 1.0)).astype(dt)
    o = o + jnp.einsum('bihd,bhdv->bihv', q, state) * cross[None, :, :, None]
    kdec = jnp.exp(log_gamma[None, :] * (C - 1.0 - pos)[:, None]).astype(dt)
    total = jnp.exp(log_gamma * C).astype(dt)
    new_state = total[None, :, None, None] * state + jnp.einsum('bjhd,bjhv->bhdv', k * kdec[None, :, :, None], v)
    return o, new_state


def retention_prompt(q, k, v, log_gamma):
    B, T, H, dk = q.shape
    dv = v.shape[-1]
    nc = T // CHUNK
    to_chunks = lambda a: jnp.moveaxis(a.reshape((B, nc, CHUNK) + a.shape[2:]), 1, 0)
    s0 = jnp.zeros((B, H, dk, dv), q.dtype)

    def step(s, inp):
        qc, kc, vc = inp
        o, s_new = retention_chunk(qc, kc, vc, s, log_gamma)
        return s_new, o

    s_fin, o = lax.scan(step, s0, (to_chunks(q), to_chunks(k), to_chunks(v)))
    return jnp.moveaxis(o, 0, 1).reshape(B, T, H, dv), s_fin


def hier_moe(h, w_group, w_router, w_gate_e, w_up_e, w_down_e):
    g_prob = jax.nn.softmax(jnp.einsum('btd,dg->btg', h, w_group).astype(jnp.float32), axis=-1)
    g_top, g_idx = lax.top_k(g_prob, 1)
    e_logits_all = jnp.einsum('btd,gde->btge', h, w_router).astype(jnp.float32)
    e_logits = jnp.take_along_axis(e_logits_all, g_idx[..., None], axis=2)[:, :, 0]
    e_top, e_idx = lax.top_k(jax.nn.softmax(e_logits, axis=-1), 2)
    gate = g_top * (e_top / jnp.sum(e_top, axis=-1, keepdims=True))
    expert_id = g_idx * EXPERTS_PER_GROUP + e_idx
    combine = jnp.sum(jax.nn.one_hot(expert_id, N_EXPERTS, dtype=jnp.float32) * gate[..., None], axis=2).astype(h.dtype)
    out = jnp.zeros_like(h)
    for e in range(N_EXPERTS):
        a = h @ w_gate_e[e]
        b = h @ w_up_e[e]
        out = out + combine[..., e:e + 1] * ((jax.nn.silu(a) * b) @ w_down_e[e])
    return out


def trunk_layer(x, c, past, slopes, log_gamma, w_ada, b_ada, norm1_g, norm2_g, w_in, q_norm_g, k_norm_g,
                w_out, w_group, w_router, w_gate_e, w_up_e, w_down_e):
    B, T, _ = x.shape
    mod = jax.nn.silu(c) @ w_ada + b_ada
    shift1, scale1, gate1, shift2, scale2, gate2 = [m[:, None, :] for m in jnp.split(mod, 6, axis=-1)]
    h = rms_norm(x, norm1_g) * (1.0 + scale1) + shift1
    z = h @ w_in
    aq, ak, av, iq, ik, iw, rq, rk, rv, rg = jnp.split(z, IN_SPLIT_POINTS, axis=-1)
    aq = rms_norm(aq.reshape(B, T, A_HEADS, A_HEAD_DIM), q_norm_g)
    ak = rms_norm(ak.reshape(B, T, A_HEADS, A_HEAD_DIM), k_norm_g)
    av = av.reshape(B, T, A_HEADS, A_HEAD_DIM)
    iq = iq.reshape(B, T, IDX_HEADS, IDX_DIM)
    rq = rq.reshape(B, T, B_HEADS, B_HEAD_DIM)
    rk = rk.reshape(B, T, B_HEADS, B_HEAD_DIM) * RET_K_SCALE
    rv = rv.reshape(B, T, B_HEADS, B_HEAD_DIM)
    if past is None:
        attn = attention_prompt(aq, ak, av, iq, ik, iw, slopes)
        ret, ret_state = retention_prompt(rq, rk, rv, log_gamma)
    else:
        ck, cv, cki, st = past
        attn = attention_sample(aq, ak, av, iq, ik, iw, ck, cv, cki, slopes)
        ret, ret_state = retention_chunk(rq, rk, rv, st, log_gamma)
    ret = rms_norm(ret).reshape(B, T, B_WIDTH) * jax.nn.silu(rg)
    mix = jnp.concatenate([attn.reshape(B, T, A_WIDTH), ret], axis=-1) @ w_out
    x = x + gate1 * mix
    h2 = rms_norm(x, norm2_g) * (1.0 + scale2) + shift2
    x = x + gate2 * hier_moe(h2, w_group, w_router, w_gate_e, w_up_e, w_down_e)
    return x, ak, av, ik, ret_state


def setup_inputs(seed: int = 0) -> dict:
    key = jax.random.key(seed)
    ks = jax.random.split(key, 24)
    nrm = lambda k, shape, s: jax.random.normal(k, shape, jnp.float32) * s
    D = D_MODEL
    return {
        'x_prompt': nrm(ks[0], (BATCH, SEQ, D), 1.0),
        'x_sample': nrm(ks[1], (DEC_BATCH, DEC_SEQ, D), 1.0),
        'cache_k': nrm(ks[2], (DEPTH, DEC_BATCH, PAST_LEN, A_HEADS, A_HEAD_DIM), 1.0),
        'cache_v': nrm(ks[3], (DEPTH, DEC_BATCH, PAST_LEN, A_HEADS, A_HEAD_DIM), 1.0),
        'cache_kidx': nrm(ks[4], (DEPTH, DEC_BATCH, PAST_LEN, IDX_DIM), 1.0),
        'state_ret': nrm(ks[5], (DEPTH, DEC_BATCH, B_HEADS, B_HEAD_DIM, B_HEAD_DIM), 1.0),
        'c_prompt': nrm(ks[6], (BATCH, D), 1.0),
        'c_sample': nrm(ks[7], (DEC_BATCH, D), 1.0),
        'w_ada': nrm(ks[8], (DEPTH, D, 6 * D), 0.5 * D ** -0.5),
        'b_ada': nrm(ks[9], (DEPTH, 6 * D), 0.02),
        'norm1_g': 1.0 + nrm(ks[10], (DEPTH, D), 0.02),
        'norm2_g': 1.0 + nrm(ks[11], (DEPTH, D), 0.02),
        'w_in': nrm(ks[12], (DEPTH, D, IN_WIDTH), D ** -0.5),
        'q_norm_g': 1.0 + nrm(ks[13], (DEPTH, A_HEAD_DIM), 0.02),
        'k_norm_g': 1.0 + nrm(ks[14], (DEPTH, A_HEAD_DIM), 0.02),
        'w_out': nrm(ks[15], (DEPTH, MIX_WIDTH, D), MIX_WIDTH ** -0.5),
        'w_group': nrm(ks[16], (DEPTH, D, N_GROUPS), D ** -0.5),
        'w_router': nrm(ks[17], (DEPTH, N_GROUPS, D, EXPERTS_PER_GROUP), D ** -0.5),
        'w_gate_e': nrm(ks[18], (DEPTH, N_EXPERTS, D, D_EXPERT), D ** -0.5),
        'w_up_e': nrm(ks[19], (DEPTH, N_EXPERTS, D, D_EXPERT), D ** -0.5),
        'w_down_e': nrm(ks[20], (DEPTH, N_EXPERTS, D_EXPERT, D), D_EXPERT ** -0.5),
    }


def reference(x_prompt, x_sample, cache_k, cache_v, cache_kidx, state_ret, c_prompt, c_sample,
              w_ada, b_ada, norm1_g, norm2_g, w_in, q_norm_g, k_norm_g, w_out,
              w_group, w_router, w_gate_e, w_up_e, w_down_e):
    slopes = alibi_slopes()
    log_gamma = retention_log_decay()
    xp, xs = x_prompt, x_sample
    kp_l, vp_l, kip_l, sp_l = [], [], [], []
    ks_l, vs_l, kis_l, ss_l = [], [], [], []
    for l in range(DEPTH):
        weights = (w_ada[l], b_ada[l], norm1_g[l], norm2_g[l], w_in[l], q_norm_g[l], k_norm_g[l],
                   w_out[l], w_group[l], w_router[l], w_gate_e[l], w_up_e[l], w_down_e[l])
        xp, kp, vp, kip, sp = trunk_layer(xp, c_prompt, None, slopes, log_gamma, *weights)
        xs, kn, vn, kin, sn = trunk_layer(xs, c_sample, (cache_k[l], cache_v[l], cache_kidx[l], state_ret[l]),
                                          slopes, log_gamma, *weights)
        kp_l.append(kp); vp_l.append(vp); kip_l.append(kip); sp_l.append(sp)
        ks_l.append(kn); vs_l.append(vn); kis_l.append(kin); ss_l.append(sn)
    return (xp, xs, jnp.stack(kp_l), jnp.stack(vp_l), jnp.stack(kip_l), jnp.stack(sp_l),
            jnp.stack(ks_l), jnp.stack(vs_l), jnp.stack(kis_l), jnp.stack(ss_l))
```

```python
import functools
import math

import numpy as np
import jax
import jax.numpy as jnp
from jax import lax
from jax.experimental import pallas as pl
from jax.experimental.pallas import tpu as pltpu

F32 = jnp.float32
BF16 = jnp.bfloat16
I32 = jnp.int32

D_MODEL = 2048
CHUNK = 64
N_HEADS = 8
HEAD_DIM = 128
WIDTH = N_HEADS * HEAD_DIM
IDX_HEADS = 8
IDX_DIM = 64
TOPK_MAX = 256
ATTN_SCALE = HEAD_DIM ** -0.5
IDX_SCALE = IDX_DIM ** -0.5
IDX_W_SCALE = IDX_HEADS ** -0.5
RET_K_SCALE = HEAD_DIM ** -0.5
N_GROUPS = 4
EXPERTS_PER_GROUP = 4
N_EXPERTS = N_GROUPS * EXPERTS_PER_GROUP
D_EXPERT = 512
EPS = 1e-6

OFF_IQ = 3 * WIDTH
IDX_COLS = IDX_HEADS * IDX_DIM + IDX_DIM + IDX_HEADS
OFF_RQ = OFF_IQ + IDX_COLS
N_COL_GROUPS = 8

LANES = 128
INT_MIN = -(2 ** 31)
TAKE_ALL = 2 ** 30
MASKED_DIST = 1e33
VMEM_LIMIT = 56 * 1024 * 1024

ALIBI_SLOPES = [float(2.0 ** (-8.0 * (h + 1) / N_HEADS)) for h in range(N_HEADS)]
LOG_GAMMA = [float(np.log1p(-(2.0 ** (-5.0 - h)))) for h in range(N_HEADS)]


def _params(sem):
    return pltpu.CompilerParams(dimension_semantics=sem, vmem_limit_bytes=VMEM_LIMIT)


def _silu(x):
    return x * jax.nn.sigmoid(x)


def _mod_kernel(c_ref, w_ref, b_ref, o_ref):
    s = _silu(c_ref[...])
    o_ref[...] = jnp.dot(s, w_ref[...], precision=lax.Precision.HIGHEST,
                         preferred_element_type=F32) + b_ref[...]


def _adaln_mod(c, w_ada, b_ada):
    R, D = c.shape
    N = w_ada.shape[1]
    tn = 1024
    return pl.pallas_call(
        _mod_kernel,
        grid=(N // tn,),
        in_specs=[pl.BlockSpec((R, D), lambda j: (0, 0)),
                  pl.BlockSpec((D, tn), lambda j: (0, j)),
                  pl.BlockSpec((1, tn), lambda j: (0, j))],
        out_specs=pl.BlockSpec((R, tn), lambda j: (0, j)),
        out_shape=jax.ShapeDtypeStruct((R, N), F32),
        compiler_params=_params(("arbitrary",)),
        name="adaln_mod",
    )(c, w_ada, b_ada.reshape(1, N))


def _inproj_kernel(x_ref, shift_ref, scale_ref, g1_ref, w_ref, qg_ref, kg_ref,
                   aq_ref, ak_ref, akb_ref, av_ref, avb_ref, iq_ref, ik_ref, ikb_ref, iw_ref,
                   rq_ref, rk_ref, rv_ref, rg_ref, h_scr, *, bb, tt):
    j = pl.program_id(1)
    D = x_ref.shape[-1]

    @pl.when(j == 0)
    def _():
        x = x_ref[...]
        ms = jnp.mean(x * x, axis=-1, keepdims=True)
        y = x * lax.rsqrt(ms + EPS) * g1_ref[...]
        y = y.reshape(bb, tt, D) * (1.0 + scale_ref[...]) + shift_ref[...]
        h_scr[...] = y.reshape(bb * tt, D).astype(BF16)

    z = jnp.dot(h_scr[...], w_ref[...], preferred_element_type=F32)

    def head_rms(g_ref, post):
        for h in range(N_HEADS):
            sl = slice(h * HEAD_DIM, (h + 1) * HEAD_DIM)
            zh = z[:, sl]
            ms = jnp.mean(zh * zh, axis=-1, keepdims=True)
            post(sl, zh * lax.rsqrt(ms + EPS) * g_ref[...])

    @pl.when(j == 0)
    def _():
        def post(sl, v):
            aq_ref[:, sl] = (v * ATTN_SCALE).astype(BF16)
        head_rms(qg_ref, post)

    @pl.when(j == 1)
    def _():
        def post(sl, v):
            ak_ref[:, sl] = v
            akb_ref[:, sl] = v.astype(BF16)
        head_rms(kg_ref, post)

    @pl.when(j == 2)
    def _():
        av_ref[...] = z
        avb_ref[...] = z.astype(BF16)

    @pl.when(j == 3)
    def _():
        nq = IDX_HEADS * IDX_DIM
        iq_ref[...] = (z[:, :nq] * IDX_SCALE).astype(BF16)
        ik = z[:, nq:nq + IDX_DIM]
        ik_ref[...] = ik
        ikb_ref[...] = ik.astype(BF16)
        iw_ref[...] = z[:, nq + IDX_DIM:nq + IDX_DIM + IDX_HEADS] * IDX_W_SCALE

    @pl.when(j == 4)
    def _():
        rq_ref[...] = z.astype(BF16)

    @pl.when(j == 5)
    def _():
        rk_ref[...] = (z * RET_K_SCALE).astype(BF16)

    @pl.when(j == 6)
    def _():
        rv_ref[...] = z.astype(BF16)

    @pl.when(j == 7)
    def _():
        rg_ref[...] = z


def _inproj(x2d, shift, scale, g1, w_p, qg, kg, *, bb, tt):
    N, D = x2d.shape
    tm = bb * tt
    nI = N // tm
    tok = lambda w: pl.BlockSpec((tm, w), lambda i, j: (i, 0))
    mod = pl.BlockSpec((bb, 1, D), lambda i, j: (i if bb > 1 else 0, 0, 0))
    row = lambda w: pl.BlockSpec((1, w), lambda i, j: (0, 0))
    outs = [("aq", WIDTH, BF16), ("ak", WIDTH, F32), ("akb", WIDTH, BF16), ("av", WIDTH, F32),
            ("avb", WIDTH, BF16), ("iq", IDX_HEADS * IDX_DIM, BF16), ("ik", IDX_DIM, F32),
            ("ikb", IDX_DIM, BF16), ("iw", IDX_HEADS, F32), ("rq", WIDTH, BF16), ("rk", WIDTH, BF16),
            ("rv", WIDTH, BF16), ("rg", WIDTH, F32)]
    res = pl.pallas_call(
        functools.partial(_inproj_kernel, bb=bb, tt=tt),
        grid=(nI, N_COL_GROUPS),
        in_specs=[tok(D), mod, mod, row(D),
                  pl.BlockSpec((D, WIDTH), lambda i, j: (0, j)), row(HEAD_DIM), row(HEAD_DIM)],
        out_specs=[tok(w) for _, w, _ in outs],
        out_shape=[jax.ShapeDtypeStruct((N, w), dt) for _, w, dt in outs],
        scratch_shapes=[pltpu.VMEM((tm, D), BF16)],
        compiler_params=_params(("parallel", "arbitrary")),
        name="inproj",
    )(x2d, shift, scale, g1, w_p, qg, kg)
    return {name: r for (name, _, _), r in zip(outs, res)}


def _ret_kernel(q_ref, k_ref, v_ref, g_ref, s0_ref, o_ref, sn_ref, st_scr, *, C):
    c = pl.program_id(1)

    @pl.when(c == 0)
    def _():
        st_scr[...] = s0_ref[0]

    pi = lax.broadcasted_iota(I32, (C, C), 0)
    pj = lax.broadcasted_iota(I32, (C, C), 1)
    diff = (pi - pj).astype(F32)
    causal = pi >= pj
    pos = lax.broadcasted_iota(I32, (C, HEAD_DIM), 0).astype(F32)
    for h in range(N_HEADS):
        lg = LOG_GAMMA[h]
        sl = slice(h * HEAD_DIM, (h + 1) * HEAD_DIM)
        q = q_ref[:, sl]
        k = k_ref[:, sl]
        v = v_ref[:, sl]
        decay = jnp.where(causal, jnp.exp(lg * jnp.maximum(diff, 0.0)), 0.0)
        s = lax.dot_general(q, k, (((1,), (1,)), ((), ())), preferred_element_type=F32) * decay
        o = jnp.dot(s.astype(BF16), v, preferred_element_type=F32)
        st = st_scr[h]
        cross = jnp.exp(lg * (pos + 1.0))
        o = o + jnp.dot(q, st.astype(BF16), preferred_element_type=F32) * cross
        kdec = jnp.exp(lg * (C - 1.0 - pos))
        kd = (k.astype(F32) * kdec).astype(BF16)
        st_new = math.exp(lg * C) * st + lax.dot_general(
            kd, v, (((0,), (0,)), ((), ())), preferred_element_type=F32)
        st_scr[h] = st_new
        ms = jnp.mean(o * o, axis=-1, keepdims=True)
        o_ref[:, sl] = (o * lax.rsqrt(ms + EPS) * _silu(g_ref[:, sl])).astype(BF16)

    @pl.when(c == pl.num_programs(1) - 1)
    def _():
        sn_ref[0] = st_scr[...]


def _retention(rq, rk, rv, rg, state0, *, B, T, C):
    nC = T // C
    tok = pl.BlockSpec((C, WIDTH), lambda b, c: (b * nC + c, 0))
    st = pl.BlockSpec((1, N_HEADS, HEAD_DIM, HEAD_DIM), lambda b, c: (b, 0, 0, 0))
    return pl.pallas_call(
        functools.partial(_ret_kernel, C=C),
        grid=(B, nC),
        in_specs=[tok, tok, tok, tok, st],
        out_specs=[tok, st],
        out_shape=[jax.ShapeDtypeStruct((B * T, WIDTH), BF16),
                   jax.ShapeDtypeStruct((B, N_HEADS, HEAD_DIM, HEAD_DIM), F32)],
        scratch_shapes=[pltpu.VMEM((N_HEADS, HEAD_DIM, HEAD_DIM), F32)],
        compiler_params=_params(("parallel", "arbitrary")),
        name="retention",
    )(rq, rk, rv, rg, state0)


def _key_limit(q_first, q_count, P, Lreal):
    return jnp.minimum(Lreal, ((P + q_first + q_count - 1) // CHUNK + 1) * CHUNK)


def _idx_kernel(iq_ref, ik_ref, iwT_ref, keys_ref, tb_ref, nb_ref, *, TK, P, Lreal, topk):
    qb = pl.program_id(1)
    nkt = (_key_limit(qb * LANES, LANES, P, Lreal) + TK - 1) // TK
    tq = P + qb * LANES + lax.broadcasted_iota(I32, (TK, LANES), 1)
    lrow = lax.broadcasted_iota(I32, (TK, LANES), 0)

    def tile_body(kt, carry):
        r0 = pl.multiple_of(kt * TK, TK)
        ik = ik_ref[0, pl.ds(r0, TK), :]
        acc = jnp.zeros((TK, LANES), F32)
        for h in range(IDX_HEADS):
            qh = iq_ref[0, :, h * IDX_DIM:(h + 1) * IDX_DIM]
            r = lax.dot_general(ik, qh, (((1,), (1,)), ((), ())), preferred_element_type=F32)
            acc = acc + jnp.maximum(r, 0.0) * iwT_ref[0, 0, h:h + 1, :]
        l = r0 + lrow
        allowed = (l < Lreal) & ((l >> 6) <= (tq >> 6))
        bits = pltpu.bitcast(acc, I32)
        key = bits ^ ((bits >> 31) & 0x7FFFFFFF)
        keys_ref[0, 0, pl.ds(r0, TK), :] = jnp.where(allowed, key, INT_MIN)
        return carry

    lax.fori_loop(0, nkt, tile_body, 0)

    SUB = 64

    def count(pred):
        def body(kt, acc):
            r0 = pl.multiple_of(kt * TK, TK)
            for s in range(TK // SUB):
                blk = keys_ref[0, 0, pl.ds(r0 + s * SUB, SUB), :]
                acc = acc + pred(blk).astype(I32)
            return acc
        acc = lax.fori_loop(0, nkt, body, jnp.zeros((SUB, LANES), I32))
        return jnp.sum(acc, axis=0, keepdims=True)

    def pass_body(b, t_u):
        cand_u = t_u | lax.shift_left(jnp.int32(1), (31 - b).astype(I32))
        cand_s = cand_u ^ INT_MIN
        cnt = count(lambda blk: blk >= cand_s)
        return jnp.where(cnt >= topk, cand_u, t_u)

    t_u = lax.fori_loop(0, 32, pass_body, jnp.zeros((1, LANES), I32))
    t_s = t_u ^ INT_MIN
    cnt_ge = count(lambda blk: blk >= t_s)
    cnt_gt = count(lambda blk: blk > t_s)
    ties = (cnt_ge > topk) & (t_s != INT_MIN)
    n_take = jnp.where(ties, topk - cnt_gt, TAKE_ALL)
    tb_ref[0, 0] = jnp.broadcast_to(t_s, (LANES, LANES))
    nb_ref[0, 0] = jnp.broadcast_to(n_take, (LANES, LANES))


def _index_select(iq, ikb, iwT, *, TK, P, Lreal, topk):
    nB, Lp, _ = ikb.shape
    nQ = iwT.shape[1]
    tile = pl.BlockSpec((1, 1, LANES, LANES), lambda b, q: (b, q, 0, 0))
    return pl.pallas_call(
        functools.partial(_idx_kernel, TK=TK, P=P, Lreal=Lreal, topk=topk),
        grid=(nB, nQ),
        in_specs=[pl.BlockSpec((1, LANES, IDX_HEADS * IDX_DIM), lambda b, q: (b, q, 0)),
                  pl.BlockSpec((1, Lp, IDX_DIM), lambda b, q: (b, 0, 0)),
                  pl.BlockSpec((1, 1, IDX_HEADS, LANES), lambda b, q: (b, q, 0, 0))],
        out_specs=[pl.BlockSpec((1, 1, Lp, LANES), lambda b, q: (b, q, 0, 0)), tile, tile],
        out_shape=[jax.ShapeDtypeStruct((nB, nQ, Lp, LANES), I32),
                   jax.ShapeDtypeStruct((nB, nQ, LANES, LANES), I32),
                   jax.ShapeDtypeStruct((nB, nQ, LANES, LANES), I32)],
        compiler_params=_params(("parallel", "arbitrary")),
        name="index_select",
    )(iq, ikb, iwT)


def _attn_kernel(qbs_ref, kts_ref, last_ref, q_ref, k_ref, v_ref, keys_ref, tb_ref, nb_ref, o_ref,
                 ndm_scr, m_scr, l_scr, acc_scr, tie_scr, *, TQ, TK, P, Lreal):
    s_id = pl.program_id(1)
    qb = qbs_ref[s_id]
    kt = kts_ref[s_id]
    R = TQ // LANES

    @pl.when(kt == 0)
    def _():
        m_scr[...] = jnp.full(m_scr.shape, -jnp.inf, F32)
        l_scr[...] = jnp.zeros(l_scr.shape, F32)
        acc_scr[...] = jnp.zeros(acc_scr.shape, F32)
        tie_scr[...] = jnp.zeros(tie_scr.shape, F32)

    keys = jnp.concatenate([keys_ref[0, r].T for r in range(R)], axis=0)
    thr = jnp.concatenate([tb_ref[0, r].T for r in range(R)], axis=0)
    ntk = jnp.concatenate([nb_ref[0, r].T for r in range(R)], axis=0)
    thr = jnp.concatenate([thr] * (TK // LANES), axis=1)
    l = kt * TK + lax.broadcasted_iota(I32, (TQ, TK), 1)
    tq = P + qb * TQ + lax.broadcasted_iota(I32, (TQ, TK), 0)
    allowed = (l < Lreal) & ((l >> 6) <= (tq >> 6))
    negdist = -jnp.abs(tq - l).astype(F32)
    has_ties = jnp.max(jnp.where(ntk == TAKE_ALL, 0, 1)) > 0

    @pl.when(jnp.logical_not(has_ties))
    def _():
        ndm_scr[...] = jnp.where((keys >= thr) & allowed, negdist, -MASKED_DIST)

    @pl.when(has_ties)
    def _():
        eq = keys == thr
        ui = lax.broadcasted_iota(I32, (TK, TK), 0)
        uj = lax.broadcasted_iota(I32, (TK, TK), 1)
        upper = jnp.where(ui <= uj, 1.0, 0.0).astype(BF16)
        prefix = jnp.dot(jnp.where(eq, 1.0, 0.0).astype(BF16), upper, preferred_element_type=F32)
        rank = tie_scr[:, 0:1] + prefix
        ntk_f = jnp.concatenate([ntk] * (TK // LANES), axis=1).astype(F32)
        sel = ((keys > thr) | (eq & (rank <= ntk_f))) & allowed
        ndm_scr[...] = jnp.where(sel, negdist, -MASKED_DIST)
        tie_scr[...] = tie_scr[...] + prefix[:, TK - 1:TK]

    for h in range(N_HEADS):
        sl = slice(h * HEAD_DIM, (h + 1) * HEAD_DIM)
        s = lax.dot_general(q_ref[0, :, sl], k_ref[0, :, sl], (((1,), (1,)), ((), ())),
                            preferred_element_type=F32)
        a = s + ALIBI_SLOPES[h] * ndm_scr[...]
        m_prev = m_scr[h]
        m_new = jnp.maximum(m_prev, jnp.max(a, axis=-1, keepdims=True))
        alpha = jnp.exp(m_prev - m_new)
        p = jnp.exp(a - m_new)
        l_scr[h] = alpha * l_scr[h] + jnp.sum(p, axis=-1, keepdims=True)
        acc_scr[:, sl] = alpha * acc_scr[:, sl] + jnp.dot(p.astype(BF16), v_ref[0, :, sl],
                                                          preferred_element_type=F32)
        m_scr[h] = m_new

    @pl.when(last_ref[s_id] == 1)
    def _():
        for h in range(N_HEADS):
            sl = slice(h * HEAD_DIM, (h + 1) * HEAD_DIM)
            o_ref[0, :, sl] = (acc_scr[:, sl] / l_scr[h]).astype(BF16)


def _attention(aq, kb, vb, keysT, tb, nb, *, TQ, TK, P, Lreal):
    nB, Tq, _ = aq.shape
    nQ = Tq // TQ
    R = TQ // LANES
    qbs, kts, last = [], [], []
    for qb in range(nQ):
        limit = min(Lreal, ((P + qb * TQ + TQ - 1) // CHUNK + 1) * CHUNK)
        n = -(-limit // TK)
        qbs += [qb] * n
        kts += list(range(n))
        last += [0] * (n - 1) + [1]
    steps = len(qbs)
    qmap = lambda b, s, qbs, kts, last: (b, qbs[s], 0)
    kmap = lambda b, s, qbs, kts, last: (b, kts[s], 0)
    grid_spec = pltpu.PrefetchScalarGridSpec(
        num_scalar_prefetch=3,
        grid=(nB, steps),
        in_specs=[pl.BlockSpec((1, TQ, WIDTH), qmap),
                  pl.BlockSpec((1, TK, WIDTH), kmap),
                  pl.BlockSpec((1, TK, WIDTH), kmap),
                  pl.BlockSpec((1, R, TK, LANES), lambda b, s, qbs, kts, last: (b, qbs[s], kts[s], 0)),
                  pl.BlockSpec((1, R, LANES, LANES), lambda b, s, qbs, kts, last: (b, qbs[s], 0, 0)),
                  pl.BlockSpec((1, R, LANES, LANES), lambda b, s, qbs, kts, last: (b, qbs[s], 0, 0))],
        out_specs=pl.BlockSpec((1, TQ, WIDTH), qmap),
        scratch_shapes=[pltpu.VMEM((TQ, TK), F32),
                        pltpu.VMEM((N_HEADS, TQ, 1), F32),
                        pltpu.VMEM((N_HEADS, TQ, 1), F32),
                        pltpu.VMEM((TQ, WIDTH), F32),
                        pltpu.VMEM((TQ, LANES), F32)])
    arr = lambda v: jnp.asarray(np.array(v, np.int32))
    return pl.pallas_call(
        functools.partial(_attn_kernel, TQ=TQ, TK=TK, P=P, Lreal=Lreal),
        grid_spec=grid_spec,
        out_shape=jax.ShapeDtypeStruct((nB, Tq, WIDTH), BF16),
        compiler_params=_params(("parallel", "arbitrary")),
        name="sparse_attention",
    )(arr(qbs), arr(kts), arr(last), aq, kb, vb, keysT, tb, nb)


def _outproj_kernel(attn_ref, ret_ref, x_ref, gate_ref, shift_ref, scale_ref, g2_ref, wo_ref, wr_ref,
                    x1_ref, h2_ref, comb_ref, *, bb, tt):
    D = x_ref.shape[-1]
    mix = (jnp.dot(attn_ref[...], wo_ref[0:WIDTH, :], preferred_element_type=F32)
           + jnp.dot(ret_ref[...], wo_ref[WIDTH:2 * WIDTH, :], preferred_element_type=F32))
    x1 = x_ref[...].reshape(bb, tt, D) + gate_ref[...] * mix.reshape(bb, tt, D)
    x1_ref[...] = x1.reshape(bb * tt, D)
    ms = jnp.mean(x1 * x1, axis=-1, keepdims=True)
    h2 = x1 * lax.rsqrt(ms + EPS) * g2_ref[...] * (1.0 + scale_ref[...]) + shift_ref[...]
    h2 = h2.reshape(bb * tt, D)
    h2_ref[...] = h2.astype(BF16)

    logits = jnp.dot(h2, wr_ref[...], precision=lax.Precision.HIGHEST, preferred_element_type=F32)
    lane = lax.broadcasted_iota(I32, logits.shape, 1)
    ninf = -jnp.inf
    rmax = lambda v: jnp.max(v, axis=-1, keepdims=True)
    rsum = lambda v: jnp.sum(v, axis=-1, keepdims=True)
    first = lambda m: jnp.min(jnp.where(m, lane, LANES), axis=-1, keepdims=True)
    gl = jnp.where(lane < N_GROUPS, logits, ninf)
    gmax = rmax(gl)
    g_top = 1.0 / rsum(jnp.exp(gl - gmax))
    g_idx = first(gl == gmax)
    emask = (lane >= N_GROUPS) & (lane < N_GROUPS + N_EXPERTS) & (((lane - N_GROUPS) >> 2) == g_idx)
    el = jnp.where(emask, logits, ninf)
    emax = rmax(el)
    esum = rsum(jnp.exp(el - emax))
    i1 = first(el == emax)
    el2 = jnp.where(lane == i1, ninf, el)
    emax2 = rmax(el2)
    i2 = first(el2 == emax2)
    p1 = 1.0 / esum
    p2 = jnp.exp(emax2 - emax) / esum
    den = p1 + p2
    comb_ref[...] = jnp.where(lane == i1, g_top * (p1 / den),
                              jnp.where(lane == i2, g_top * (p2 / den), 0.0))


def _outproj(attn, ret, x2d, gate1, shift2, scale2, g2, wo, wr, *, bb, tt):
    N, D = x2d.shape
    tm = bb * tt
    tok = lambda w: pl.BlockSpec((tm, w), lambda i: (i, 0))
    mod = pl.BlockSpec((bb, 1, D), lambda i: (i if bb > 1 else 0, 0, 0))
    full = lambda a: pl.BlockSpec(a.shape, lambda i: (0,) * a.ndim)
    return pl.pallas_call(
        functools.partial(_outproj_kernel, bb=bb, tt=tt),
        grid=(N // tm,),
        in_specs=[tok(WIDTH), tok(WIDTH), tok(D), mod, mod, mod, full(g2), full(wo), full(wr)],
        out_specs=[tok(D), tok(D), tok(LANES)],
        out_shape=[jax.ShapeDtypeStruct((N, D), F32), jax.ShapeDtypeStruct((N, D), BF16),
                   jax.ShapeDtypeStruct((N, LANES), F32)],
        compiler_params=_params(("parallel",)),
        name="outproj_router",
    )(attn, ret, x2d, gate1, shift2, scale2, g2, wo, wr)


def _moe_kernel(h_ref, comb_ref, x1_ref, gate_ref, wg_ref, wu_ref, wd_ref, y_ref, acc_scr, *, bb, tt):
    e = pl.program_id(1)
    D = x1_ref.shape[-1]

    @pl.when(e == 0)
    def _():
        acc_scr[...] = jnp.zeros(acc_scr.shape, F32)

    h = h_ref[...]
    a = jnp.dot(h, wg_ref[0], preferred_element_type=F32)
    b = jnp.dot(h, wu_ref[0], preferred_element_type=F32)
    mid = (_silu(a) * b).astype(BF16)
    out = jnp.dot(mid, wd_ref[0], preferred_element_type=F32)
    lane = lax.broadcasted_iota(I32, comb_ref.shape, 1)
    w = jnp.sum(jnp.where(lane == e + N_GROUPS, comb_ref[...], 0.0), axis=-1, keepdims=True)
    acc_scr[...] += w * out

    @pl.when(e == N_EXPERTS - 1)
    def _():
        y = x1_ref[...].reshape(bb, tt, D) + gate_ref[...] * acc_scr[...].reshape(bb, tt, D)
        y_ref[...] = y.reshape(bb * tt, D)


def _moe(h2, comb, x1, gate2, wg, wu, wd, *, bb, tt):
    N, D = x1.shape
    tm = bb * tt
    tok = lambda w: pl.BlockSpec((tm, w), lambda i, e: (i, 0))
    mod = pl.BlockSpec((bb, 1, D), lambda i, e: (i if bb > 1 else 0, 0, 0))
    return pl.pallas_call(
        functools.partial(_moe_kernel, bb=bb, tt=tt),
        grid=(N // tm, N_EXPERTS),
        in_specs=[tok(D), tok(LANES), tok(D), mod,
                  pl.BlockSpec((1, D, D_EXPERT), lambda i, e: (e, 0, 0)),
                  pl.BlockSpec((1, D, D_EXPERT), lambda i, e: (e, 0, 0)),
                  pl.BlockSpec((1, D_EXPERT, D), lambda i, e: (e, 0, 0))],
        out_specs=tok(D),
        out_shape=jax.ShapeDtypeStruct((N, D), F32),
        scratch_shapes=[pltpu.VMEM((tm, D), F32)],
        compiler_params=_params(("parallel", "arbitrary")),
        name="moe",
    )(h2, comb, x1, gate2, wg, wu, wd)


def _layer(x, mod, past, W, *, bb, tt, TQ, TK, ret_chunk):
    B, T, D = x.shape
    N = B * T
    x2d = x.reshape(N, D)
    shift1, scale1, gate1, shift2, scale2, gate2 = [m.reshape(B, 1, D) for m in jnp.split(mod, 6, axis=-1)]
    z = _inproj(x2d, shift1, scale1, W["g1"], W["w_in"], W["qg"], W["kg"], bb=bb, tt=tt)

    if past is None:
        state0 = jnp.zeros((B, N_HEADS, HEAD_DIM, HEAD_DIM), F32)
        P, Lreal, nB = 0, T, B
        topk = min(TOPK_MAX, T // 4)
        kb = z["akb"].reshape(B, T, WIDTH)
        vb = z["avb"].reshape(B, T, WIDTH)
        ikb = z["ikb"].reshape(B, T, IDX_DIM)
        Tq = T
        pad_q = lambda a: a.reshape(B, T, a.shape[-1])
    else:
        ck, cv, cki, state0 = past
        P = ck.shape[1]
        Lreal = P + T
        topk = min(TOPK_MAX, Lreal // 4)
        Lp = -(-Lreal // TK) * TK
        cat = lambda c, n, w: jnp.concatenate(
            [c.reshape(B, P, w).astype(BF16), n.reshape(B, T, w), jnp.zeros((B, Lp - Lreal, w), BF16)], axis=1)
        kb = cat(ck, z["akb"], WIDTH)
        vb = cat(cv, z["avb"], WIDTH)
        ikb = cat(cki, z["ikb"], IDX_DIM)
        Tq = -(-T // TQ) * TQ
        pad_q = lambda a: jnp.pad(a.reshape(B, T, a.shape[-1]), ((0, 0), (0, Tq - T), (0, 0)))

    iq = pad_q(z["iq"])
    iwT = pad_q(z["iw"]).reshape(B, Tq // LANES, LANES, IDX_HEADS).transpose(0, 1, 3, 2)
    keysT, tb, nb = _index_select(iq, ikb, iwT, TK=TK, P=P, Lreal=Lreal, topk=topk)
    attn = _attention(pad_q(z["aq"]), kb, vb, keysT, tb, nb, TQ=TQ, TK=TK, P=P, Lreal=Lreal)
    attn = attn[:, :T].reshape(N, WIDTH)

    ret, ret_state = _retention(z["rq"], z["rk"], z["rv"], z["rg"], state0, B=B, T=T, C=ret_chunk)
    x1, h2, comb = _outproj(attn, ret, x2d, gate1, shift2, scale2, W["g2"], W["w_out"], W["w_r"], bb=bb, tt=tt)
    y = _moe(h2, comb, x1, gate2, W["wg"], W["wu"], W["wd"], bb=bb, tt=tt)
    return (y.reshape(B, T, D), z["ak"].reshape(B, T, N_HEADS, HEAD_DIM),
            z["av"].reshape(B, T, N_HEADS, HEAD_DIM), z["ik"].reshape(B, T, IDX_DIM), ret_state)


def kernel(x_prompt, x_sample, cache_k, cache_v, cache_kidx, state_ret, c_prompt, c_sample, w_ada, b_ada,
           norm1_g, norm2_g, w_in, q_norm_g, k_norm_g, w_out, w_group, w_router, w_gate_e, w_up_e, w_down_e):
    depth = w_ada.shape[0]
    D = D_MODEL
    Bp, Bs = x_prompt.shape[0], x_sample.shape[0]
    Ts = x_sample.shape[1]
    xp, xs = x_prompt, x_sample
    outs_p, outs_s = [], []
    for l in range(depth):
        pad_cols = N_COL_GROUPS * WIDTH - w_in.shape[-1]
        w_r = jnp.concatenate(
            [w_group[l], jnp.moveaxis(w_router[l], 0, 1).reshape(D, N_EXPERTS),
             jnp.zeros((D, LANES - N_GROUPS - N_EXPERTS), F32)], axis=1)
        W = {
            "g1": norm1_g[l].reshape(1, D), "g2": norm2_g[l].reshape(1, D),
            "qg": q_norm_g[l].reshape(1, HEAD_DIM), "kg": k_norm_g[l].reshape(1, HEAD_DIM),
            "w_in": jnp.concatenate([w_in[l][:, :OFF_RQ], jnp.zeros((D, pad_cols), F32),
                                     w_in[l][:, OFF_RQ:]], axis=1).astype(BF16),
            "w_out": w_out[l].astype(BF16), "w_r": w_r,
            "wg": w_gate_e[l].astype(BF16), "wu": w_up_e[l].astype(BF16), "wd": w_down_e[l].astype(BF16),
        }
        rows = Bp + Bs
        rows_p = -(-rows // 8) * 8
        c_all = jnp.concatenate([c_prompt, c_sample, jnp.zeros((rows_p - rows, D), F32)], axis=0)
        mod = _adaln_mod(c_all, w_ada[l], b_ada[l])
        past = (cache_k[l], cache_v[l], cache_kidx[l], state_ret[l])
        xp, kp, vp, kip, sp = _layer(xp, mod[:Bp], None, W, bb=1, tt=512, TQ=256, TK=512, ret_chunk=256)
        xs, kn, vn, kin, sn = _layer(xs, mod[Bp:rows], past, W, bb=512 // Ts, tt=Ts, TQ=128, TK=384,
                                     ret_chunk=Ts)
        outs_p.append((kp, vp, kip, sp))
        outs_s.append((kn, vn, kin, sn))
    st = lambda xs_, i: jnp.stack([o[i] for o in xs_])
    return (xp, xs, st(outs_p, 0), st(outs_p, 1), st(outs_p, 2), st(outs_p, 3),
            st(outs_s, 0), st(outs_s, 1), st(outs_s, 2), st(outs_s, 3))
```

```python
import functools
import math

import numpy as np
import jax
import jax.numpy as jnp
from jax import lax
from jax.experimental import pallas as pl
from jax.experimental.pallas import tpu as pltpu

F32 = jnp.float32
BF16 = jnp.bfloat16
I32 = jnp.int32

D_MODEL = 2048
CHUNK = 64
N_HEADS = 8
HEAD_DIM = 128
WIDTH = N_HEADS * HEAD_DIM
IDX_HEADS = 8
IDX_DIM = 64
TOPK_MAX = 256
ATTN_SCALE = HEAD_DIM ** -0.5
IDX_SCALE = IDX_DIM ** -0.5
IDX_W_SCALE = IDX_HEADS ** -0.5
RET_K_SCALE = HEAD_DIM ** -0.5
N_GROUPS = 4
EXPERTS_PER_GROUP = 4
N_EXPERTS = N_GROUPS * EXPERTS_PER_GROUP
D_EXPERT = 512
EPS = 1e-6

OFF_IQ = 3 * WIDTH
IDX_COLS = IDX_HEADS * IDX_DIM + IDX_DIM + IDX_HEADS
OFF_RQ = OFF_IQ + IDX_COLS
N_COL_GROUPS = 8

LANES = 128
INT_MIN = -(2 ** 31)
TAKE_ALL = 2 ** 30
MASKED_DIST = 1e33
VMEM_LIMIT = 56 * 1024 * 1024

LOG2E = 1.4426950408889634
VT_ROWS = HEAD_DIM + 16
ALIBI_SLOPES = [float(2.0 ** (-8.0 * (h + 1) / N_HEADS)) for h in range(N_HEADS)]
LOG_GAMMA = [float(np.log1p(-(2.0 ** (-5.0 - h)))) for h in range(N_HEADS)]


def _params(sem):
    return pltpu.CompilerParams(dimension_semantics=sem, vmem_limit_bytes=VMEM_LIMIT)


def _silu(x):
    return x * jax.nn.sigmoid(x)


def _mod_kernel(c_ref, w_ref, b_ref, o_ref):
    s = _silu(c_ref[...])
    o_ref[...] = jnp.dot(s, w_ref[...], precision=lax.Precision.HIGHEST,
                         preferred_element_type=F32) + b_ref[...]


def _adaln_mod(c, w_ada, b_ada):
    R, D = c.shape
    N = w_ada.shape[1]
    tn = 1024
    return pl.pallas_call(
        _mod_kernel,
        grid=(N // tn,),
        in_specs=[pl.BlockSpec((R, D), lambda j: (0, 0)),
                  pl.BlockSpec((D, tn), lambda j: (0, j)),
                  pl.BlockSpec((1, tn), lambda j: (0, j))],
        out_specs=pl.BlockSpec((R, tn), lambda j: (0, j)),
        out_shape=jax.ShapeDtypeStruct((R, N), F32),
        compiler_params=_params(("arbitrary",)),
        name="adaln_mod",
    )(c, w_ada, b_ada.reshape(1, N))


def _inproj_kernel(x_ref, shift_ref, scale_ref, g1_ref, w_ref, qg_ref, kg_ref,
                   aq_ref, ak_ref, akb_ref, av_ref, avT_ref, iq_ref, ik_ref, ikb_ref, iw_ref,
                   rq_ref, rk_ref, rv_ref, rg_ref, h_scr, *, bb, tt):
    j = pl.program_id(1)
    D = x_ref.shape[-1]

    @pl.when(j == 0)
    def _():
        x = x_ref[...]
        ms = jnp.mean(x * x, axis=-1, keepdims=True)
        y = x * lax.rsqrt(ms + EPS) * g1_ref[...]
        y = y.reshape(bb, tt, D) * (1.0 + scale_ref[...]) + shift_ref[...]
        h_scr[...] = y.reshape(bb * tt, D).astype(BF16)

    z = jnp.dot(h_scr[...], w_ref[...], preferred_element_type=F32)

    def head_rms(g_ref, post):
        for h in range(N_HEADS):
            sl = slice(h * HEAD_DIM, (h + 1) * HEAD_DIM)
            zh = z[:, sl]
            ms = jnp.mean(zh * zh, axis=-1, keepdims=True)
            post(sl, zh * lax.rsqrt(ms + EPS) * g_ref[...])

    @pl.when(j == 0)
    def _():
        def post(sl, v):
            aq_ref[:, sl] = (v * (ATTN_SCALE * LOG2E)).astype(BF16)
        head_rms(qg_ref, post)

    @pl.when(j == 1)
    def _():
        def post(sl, v):
            ak_ref[:, sl] = v
            akb_ref[:, sl] = v.astype(BF16)
        head_rms(kg_ref, post)

    @pl.when(j == 2)
    def _():
        av_ref[...] = z
        zT = z.T
        for h in range(N_HEADS):
            avT_ref[h, 0:HEAD_DIM, :] = zT[h * HEAD_DIM:(h + 1) * HEAD_DIM, :].astype(BF16)
            avT_ref[h, HEAD_DIM:VT_ROWS, :] = jnp.ones((VT_ROWS - HEAD_DIM, zT.shape[1]), BF16)

    @pl.when(j == 3)
    def _():
        nq = IDX_HEADS * IDX_DIM
        iq_ref[...] = (z[:, :nq] * IDX_SCALE).astype(BF16)
        ik = z[:, nq:nq + IDX_DIM]
        ik_ref[...] = ik
        ikb_ref[...] = ik.astype(BF16)
        iw_ref[...] = z[:, nq + IDX_DIM:nq + IDX_DIM + IDX_HEADS] * IDX_W_SCALE

    @pl.when(j == 4)
    def _():
        rq_ref[...] = z.astype(BF16)

    @pl.when(j == 5)
    def _():
        rk_ref[...] = (z * RET_K_SCALE).astype(BF16)

    @pl.when(j == 6)
    def _():
        rv_ref[...] = z.astype(BF16)

    @pl.when(j == 7)
    def _():
        rg_ref[...] = z


def _inproj(x2d, shift, scale, g1, w_p, qg, kg, *, bb, tt):
    N, D = x2d.shape
    tm = bb * tt
    nI = N // tm
    tok = lambda w: pl.BlockSpec((tm, w), lambda i, j: (i, 0))
    mod = pl.BlockSpec((bb, 1, D), lambda i, j: (i if bb > 1 else 0, 0, 0))
    row = lambda w: pl.BlockSpec((1, w), lambda i, j: (0, 0))
    outs = [("aq", WIDTH, BF16), ("ak", WIDTH, F32), ("akb", WIDTH, BF16), ("av", WIDTH, F32),
            ("avT", None, BF16), ("iq", IDX_HEADS * IDX_DIM, BF16), ("ik", IDX_DIM, F32),
            ("ikb", IDX_DIM, BF16), ("iw", IDX_HEADS, F32), ("rq", WIDTH, BF16), ("rk", WIDTH, BF16),
            ("rv", WIDTH, BF16), ("rg", WIDTH, F32)]
    out_specs = [tok(w) if w else pl.BlockSpec((N_HEADS, VT_ROWS, tm), lambda i, j: (0, 0, i))
                 for _, w, _ in outs]
    out_shape = [jax.ShapeDtypeStruct((N, w) if w else (N_HEADS, VT_ROWS, N), dt) for _, w, dt in outs]
    res = pl.pallas_call(
        functools.partial(_inproj_kernel, bb=bb, tt=tt),
        grid=(nI, N_COL_GROUPS),
        in_specs=[tok(D), mod, mod, row(D),
                  pl.BlockSpec((D, WIDTH), lambda i, j: (0, j)), row(HEAD_DIM), row(HEAD_DIM)],
        out_specs=out_specs,
        out_shape=out_shape,
        scratch_shapes=[pltpu.VMEM((tm, D), BF16)],
        compiler_params=_params(("parallel", "arbitrary")),
        name="inproj",
    )(x2d, shift, scale, g1, w_p, qg, kg)
    return {name: r for (name, _, _), r in zip(outs, res)}


def _ret_kernel(q_ref, k_ref, v_ref, g_ref, s0_ref, o_ref, sn_ref, st_scr, *, C):
    c = pl.program_id(1)

    @pl.when(c == 0)
    def _():
        st_scr[...] = s0_ref[0]

    pi = lax.broadcasted_iota(I32, (C, C), 0)
    pj = lax.broadcasted_iota(I32, (C, C), 1)
    diff = (pi - pj).astype(F32)
    causal = pi >= pj
    pos = lax.broadcasted_iota(I32, (C, HEAD_DIM), 0).astype(F32)
    for h in range(N_HEADS):
        lg = LOG_GAMMA[h]
        sl = slice(h * HEAD_DIM, (h + 1) * HEAD_DIM)
        q = q_ref[:, sl]
        k = k_ref[:, sl]
        v = v_ref[:, sl]
        decay = jnp.where(causal, jnp.exp(lg * jnp.maximum(diff, 0.0)), 0.0)
        s = lax.dot_general(q, k, (((1,), (1,)), ((), ())), preferred_element_type=F32) * decay
        o = jnp.dot(s.astype(BF16), v, preferred_element_type=F32)
        st = st_scr[h]
        cross = jnp.exp(lg * (pos + 1.0))
        o = o + jnp.dot(q, st.astype(BF16), preferred_element_type=F32) * cross
        kdec = jnp.exp(lg * (C - 1.0 - pos))
        kd = (k.astype(F32) * kdec).astype(BF16)
        st_new = math.exp(lg * C) * st + lax.dot_general(
            kd, v, (((0,), (0,)), ((), ())), preferred_element_type=F32)
        st_scr[h] = st_new
        ms = jnp.mean(o * o, axis=-1, keepdims=True)
        o_ref[:, sl] = (o * lax.rsqrt(ms + EPS) * _silu(g_ref[:, sl])).astype(BF16)

    @pl.when(c == pl.num_programs(1) - 1)
    def _():
        sn_ref[0] = st_scr[...]


def _retention(rq, rk, rv, rg, state0, *, B, T, C):
    nC = T // C
    tok = pl.BlockSpec((C, WIDTH), lambda b, c: (b * nC + c, 0))
    st = pl.BlockSpec((1, N_HEADS, HEAD_DIM, HEAD_DIM), lambda b, c: (b, 0, 0, 0))
    return pl.pallas_call(
        functools.partial(_ret_kernel, C=C),
        grid=(B, nC),
        in_specs=[tok, tok, tok, tok, st],
        out_specs=[tok, st],
        out_shape=[jax.ShapeDtypeStruct((B * T, WIDTH), BF16),
                   jax.ShapeDtypeStruct((B, N_HEADS, HEAD_DIM, HEAD_DIM), F32)],
        scratch_shapes=[pltpu.VMEM((N_HEADS, HEAD_DIM, HEAD_DIM), F32)],
        compiler_params=_params(("parallel", "arbitrary")),
        name="retention",
    )(rq, rk, rv, rg, state0)


def _key_limit(q_first, q_count, P, Lreal):
    return jnp.minimum(Lreal, ((P + q_first + q_count - 1) // CHUNK + 1) * CHUNK)


def _idx_kernel(iq_ref, ik_ref, iwT_ref, keys_ref, tn_ref, *, TQ, TK, P, Lreal, topk):
    qb = pl.program_id(1)
    nkt = (_key_limit(qb * TQ, TQ, P, Lreal) + TK - 1) // TK
    tq = P + qb * TQ + lax.broadcasted_iota(I32, (TK, TQ), 1)
    lrow = lax.broadcasted_iota(I32, (TK, TQ), 0)

    def tile_body(kt, carry):
        r0 = pl.multiple_of(kt * TK, TK)
        ik = ik_ref[0, pl.ds(r0, TK), :]
        acc = jnp.zeros((TK, TQ), F32)
        for h in range(IDX_HEADS):
            qh = iq_ref[0, :, h * IDX_DIM:(h + 1) * IDX_DIM]
            r = lax.dot_general(ik, qh, (((1,), (1,)), ((), ())), preferred_element_type=F32)
            acc = acc + jnp.maximum(r, 0.0) * iwT_ref[0, 0, h:h + 1, :]
        l = r0 + lrow
        allowed = (l < Lreal) & ((l >> 6) <= (tq >> 6))
        bits = pltpu.bitcast(acc, I32)
        key = bits ^ ((bits >> 31) & 0x7FFFFFFF)
        keys_ref[0, 0, pl.ds(r0, TK), :] = jnp.where(allowed, key, INT_MIN)
        return carry

    lax.fori_loop(0, nkt, tile_body, 0)

    SUB = 64

    def count(pred):
        def body(kt, acc):
            r0 = pl.multiple_of(kt * TK, TK)
            for s in range(TK // SUB):
                blk = keys_ref[0, 0, pl.ds(r0 + s * SUB, SUB), :]
                acc = acc + pred(blk).astype(I32)
            return acc
        acc = lax.fori_loop(0, nkt, body, jnp.zeros((SUB, TQ), I32))
        return jnp.sum(acc, axis=0, keepdims=True)

    def pass_body(b, carry):
        t_u, cnt_t = carry
        cand_u = t_u | lax.shift_left(jnp.int32(1), (31 - b).astype(I32))
        cand_s = cand_u ^ INT_MIN
        cnt = count(lambda blk: blk >= cand_s)
        take = cnt >= topk
        return jnp.where(take, cand_u, t_u), jnp.where(take, cnt, cnt_t)

    t_u, cnt_t = lax.fori_loop(0, 32, pass_body,
                               (jnp.zeros((1, TQ), I32), jnp.full((1, TQ), TAKE_ALL, I32)))
    t_s = t_u ^ INT_MIN
    ties = (cnt_t > topk) & (t_s != INT_MIN)
    any_ties = jnp.max(ties.astype(I32)) > 0
    cnt_gt = lax.cond(any_ties, lambda: count(lambda blk: blk > t_s), lambda: jnp.zeros((1, TQ), I32))
    n_take = jnp.where(ties, topk - cnt_gt, TAKE_ALL)
    row = lax.broadcasted_iota(I32, (8, TQ), 0)
    tn_ref[0, 0] = jnp.where(row == 0, t_s, n_take)


def _index_select(iq, ikb, iwT, *, TQ, TK, P, Lreal, topk):
    nB, Lp, _ = ikb.shape
    nQ = iwT.shape[1]
    return pl.pallas_call(
        functools.partial(_idx_kernel, TQ=TQ, TK=TK, P=P, Lreal=Lreal, topk=topk),
        grid=(nB, nQ),
        in_specs=[pl.BlockSpec((1, TQ, IDX_HEADS * IDX_DIM), lambda b, q: (b, q, 0)),
                  pl.BlockSpec((1, Lp, IDX_DIM), lambda b, q: (b, 0, 0)),
                  pl.BlockSpec((1, 1, IDX_HEADS, TQ), lambda b, q: (b, q, 0, 0))],
        out_specs=[pl.BlockSpec((1, 1, Lp, TQ), lambda b, q: (b, q, 0, 0)),
                   pl.BlockSpec((1, 1, 8, TQ), lambda b, q: (b, q, 0, 0))],
        out_shape=[jax.ShapeDtypeStruct((nB, nQ, Lp, TQ), I32),
                   jax.ShapeDtypeStruct((nB, nQ, 8, TQ), I32)],
        compiler_params=_params(("parallel", "arbitrary")),
        name="index_select",
    )(iq, ikb, iwT)


def _attn_kernel(qbs_ref, kts_ref, last_ref, q_ref, k_ref, vT_ref, keys_ref, tn_ref, o_ref,
                 qT_scr, ndm_scr, a_scr, m_scr, l_scr, acc_scr, tie_scr, *, TQ, TK, KC, P, Lreal):
    s_id = pl.program_id(1)
    qb = qbs_ref[s_id]
    kt = kts_ref[s_id]
    is_last = last_ref[s_id] == 1
    heads = [slice(h * HEAD_DIM, (h + 1) * HEAD_DIM) for h in range(N_HEADS)]

    @pl.when(kt == 0)
    def _():
        m_scr[...] = jnp.full(m_scr.shape, -jnp.inf, F32)
        l_scr[...] = jnp.zeros(l_scr.shape, F32)
        acc_scr[...] = jnp.zeros(acc_scr.shape, F32)
        tie_scr[...] = jnp.zeros(tie_scr.shape, F32)
        for sl in heads:
            qT_scr[sl, :] = q_ref[0, :, sl].astype(F32).T.astype(BF16)

    keys = keys_ref[0, 0]
    thr = tn_ref[0, 0, 0:1, :]
    ntk = tn_ref[0, 0, 1:2, :]
    l = kt * TK + lax.broadcasted_iota(I32, (TK, TQ), 0)
    tq = P + qb * TQ + lax.broadcasted_iota(I32, (TK, TQ), 1)
    has_ties = jnp.max(jnp.where(ntk == TAKE_ALL, 0, 1)) > 0
    no_ties = jnp.logical_not(has_ties)

    @pl.when(jnp.logical_and(no_ties, jnp.logical_not(is_last)))
    def _():
        ndm_scr[...] = jnp.where(keys >= thr, (l - tq).astype(F32), -MASKED_DIST)

    @pl.when(jnp.logical_and(no_ties, is_last))
    def _():
        allowed = (l < Lreal) & ((l >> 6) <= (tq >> 6))
        ndm_scr[...] = jnp.where((keys >= thr) & allowed, -jnp.abs(tq - l).astype(F32), -MASKED_DIST)

    @pl.when(has_ties)
    def _():
        allowed = (l < Lreal) & ((l >> 6) <= (tq >> 6))
        eq = keys == thr
        li = lax.broadcasted_iota(I32, (TK, TK), 0)
        lj = lax.broadcasted_iota(I32, (TK, TK), 1)
        lower = jnp.where(lj <= li, 1.0, 0.0).astype(BF16)
        prefix = jnp.dot(lower, jnp.where(eq, 1.0, 0.0).astype(BF16), preferred_element_type=F32)
        rank = tie_scr[0:1, :] + prefix
        sel = ((keys > thr) | (eq & (rank <= ntk.astype(F32)))) & allowed
        ndm_scr[...] = jnp.where(sel, -jnp.abs(tq - l).astype(F32), -MASKED_DIST)
        tie_scr[0:1, :] = tie_scr[0:1, :] + prefix[TK - 1:TK, :]

    chunks = [slice(c * KC, (c + 1) * KC) for c in range(TK // KC)]

    def stage_a(h, rows, mx):
        s = jnp.dot(k_ref[0, rows, heads[h]], qT_scr[heads[h], :], preferred_element_type=F32)
        a = s + (ALIBI_SLOPES[h] * LOG2E) * ndm_scr[rows, :]
        a_scr[h % 2, rows, :] = a
        cm = jnp.max(a, axis=0, keepdims=True)
        return cm if mx is None else jnp.maximum(mx, cm)

    def stage_b(h, rows, m_new, pv):
        p = jnp.exp2(a_scr[h % 2, rows, :] - m_new).astype(BF16)
        d = jnp.dot(vT_ref[0, h, :, rows], p, preferred_element_type=F32)
        return d if pv is None else pv + d

    m_all = m_scr[...]
    l_all = l_scr[...]
    m_out, l_out = [], []
    mx = None
    for rows in chunks:
        mx = stage_a(0, rows, mx)
    for h in range(N_HEADS):
        m_prev = m_all[h:h + 1]
        m_new = jnp.maximum(m_prev, mx)
        alpha = jnp.exp2(m_prev - m_new)
        mx, pv = None, None
        for rows in chunks:
            if h + 1 < N_HEADS:
                mx = stage_a(h + 1, rows, mx)
            pv = stage_b(h, rows, m_new, pv)
        acc_scr[heads[h], :] = alpha * acc_scr[heads[h], :] + pv[0:HEAD_DIM]
        l_out.append(alpha * l_all[h:h + 1] + pv[HEAD_DIM:HEAD_DIM + 1])
        m_out.append(m_new)
    m_scr[...] = jnp.concatenate(m_out, axis=0)
    l_scr[...] = jnp.concatenate(l_out, axis=0)

    @pl.when(is_last)
    def _():
        for h in range(N_HEADS):
            o_ref[0, :, heads[h]] = (acc_scr[heads[h], :] / l_scr[h:h + 1, :]).T.astype(BF16)


def _attention(aq, kb, vT, keysT, tn, *, TQ, TK, KC, P, Lreal):
    nB, Tq, _ = aq.shape
    nQ = Tq // TQ
    qbs, kts, last = [], [], []
    for qb in range(nQ):
        limit = min(Lreal, ((P + qb * TQ + TQ - 1) // CHUNK + 1) * CHUNK)
        n = -(-limit // TK)
        assert (n - 1) * TK <= P + qb * TQ
        qbs += [qb] * n
        kts += list(range(n))
        last += [0] * (n - 1) + [1]
    steps = len(qbs)
    qmap = lambda b, s, qbs, kts, last: (b, qbs[s], 0)
    grid_spec = pltpu.PrefetchScalarGridSpec(
        num_scalar_prefetch=3,
        grid=(nB, steps),
        in_specs=[pl.BlockSpec((1, TQ, WIDTH), qmap),
                  pl.BlockSpec((1, TK, WIDTH), lambda b, s, qbs, kts, last: (b, kts[s], 0)),
                  pl.BlockSpec((1, N_HEADS, VT_ROWS, TK), lambda b, s, qbs, kts, last: (b, 0, 0, kts[s])),
                  pl.BlockSpec((1, 1, TK, TQ), lambda b, s, qbs, kts, last: (b, qbs[s], kts[s], 0)),
                  pl.BlockSpec((1, 1, 8, TQ), lambda b, s, qbs, kts, last: (b, qbs[s], 0, 0))],
        out_specs=pl.BlockSpec((1, TQ, WIDTH), qmap),
        scratch_shapes=[pltpu.VMEM((WIDTH, TQ), BF16),
                        pltpu.VMEM((TK, TQ), F32),
                        pltpu.VMEM((2, TK, TQ), F32),
                        pltpu.VMEM((N_HEADS, TQ), F32),
                        pltpu.VMEM((N_HEADS, TQ), F32),
                        pltpu.VMEM((WIDTH, TQ), F32),
                        pltpu.VMEM((8, TQ), F32)])
    arr = lambda v: jnp.asarray(np.array(v, np.int32))
    return pl.pallas_call(
        functools.partial(_attn_kernel, TQ=TQ, TK=TK, KC=KC, P=P, Lreal=Lreal),
        grid_spec=grid_spec,
        out_shape=jax.ShapeDtypeStruct((nB, Tq, WIDTH), BF16),
        compiler_params=_params(("parallel", "arbitrary")),
        name="sparse_attention",
    )(arr(qbs), arr(kts), arr(last), aq, kb, vT, keysT, tn)


def _outproj_kernel(attn_ref, ret_ref, x_ref, gate_ref, shift_ref, scale_ref, g2_ref, wo_ref, wr_ref,
                    x1_ref, h2_ref, comb_ref, *, bb, tt):
    D = x_ref.shape[-1]
    mix = (jnp.dot(attn_ref[...], wo_ref[0:WIDTH, :], preferred_element_type=F32)
           + jnp.dot(ret_ref[...], wo_ref[WIDTH:2 * WIDTH, :], preferred_element_type=F32))
    x1 = x_ref[...].reshape(bb, tt, D) + gate_ref[...] * mix.reshape(bb, tt, D)
    x1_ref[...] = x1.reshape(bb * tt, D)
    ms = jnp.mean(x1 * x1, axis=-1, keepdims=True)
    h2 = x1 * lax.rsqrt(ms + EPS) * g2_ref[...] * (1.0 + scale_ref[...]) + shift_ref[...]
    h2 = h2.reshape(bb * tt, D)
    h2_ref[...] = h2.astype(BF16)

    logits = jnp.dot(h2, wr_ref[...], precision=lax.Precision.HIGHEST, preferred_element_type=F32)
    lane = lax.broadcasted_iota(I32, logits.shape, 1)
    ninf = -jnp.inf
    rmax = lambda v: jnp.max(v, axis=-1, keepdims=True)
    rsum = lambda v: jnp.sum(v, axis=-1, keepdims=True)
    first = lambda m: jnp.min(jnp.where(m, lane, LANES), axis=-1, keepdims=True)
    gl = jnp.where(lane < N_GROUPS, logits, ninf)
    gmax = rmax(gl)
    g_top = 1.0 / rsum(jnp.exp(gl - gmax))
    g_idx = first(gl == gmax)
    emask = (lane >= N_GROUPS) & (lane < N_GROUPS + N_EXPERTS) & (((lane - N_GROUPS) >> 2) == g_idx)
    el = jnp.where(emask, logits, ninf)
    emax = rmax(el)
    esum = rsum(jnp.exp(el - emax))
    i1 = first(el == emax)
    el2 = jnp.where(lane == i1, ninf, el)
    emax2 = rmax(el2)
    i2 = first(el2 == emax2)
    p1 = 1.0 / esum
    p2 = jnp.exp(emax2 - emax) / esum
    den = p1 + p2
    comb_ref[...] = jnp.where(lane == i1, g_top * (p1 / den),
                              jnp.where(lane == i2, g_top * (p2 / den), 0.0))


def _outproj(attn, ret, x2d, gate1, shift2, scale2, g2, wo, wr, *, bb, tt):
    N, D = x2d.shape
    tm = bb * tt
    tok = lambda w: pl.BlockSpec((tm, w), lambda i: (i, 0))
    mod = pl.BlockSpec((bb, 1, D), lambda i: (i if bb > 1 else 0, 0, 0))
    full = lambda a: pl.BlockSpec(a.shape, lambda i: (0,) * a.ndim)
    return pl.pallas_call(
        functools.partial(_outproj_kernel, bb=bb, tt=tt),
        grid=(N // tm,),
        in_specs=[tok(WIDTH), tok(WIDTH), tok(D), mod, mod, mod, full(g2), full(wo), full(wr)],
        out_specs=[tok(D), tok(D), tok(LANES)],
        out_shape=[jax.ShapeDtypeStruct((N, D), F32), jax.ShapeDtypeStruct((N, D), BF16),
                   jax.ShapeDtypeStruct((N, LANES), F32)],
        compiler_params=_params(("parallel",)),
        name="outproj_router",
    )(attn, ret, x2d, gate1, shift2, scale2, g2, wo, wr)


def _moe_kernel(h_ref, comb_ref, x1_ref, gate_ref, wg_ref, wu_ref, wd_ref, y_ref, acc_scr, *, bb, tt):
    e = pl.program_id(1)
    D = x1_ref.shape[-1]

    @pl.when(e == 0)
    def _():
        acc_scr[...] = jnp.zeros(acc_scr.shape, F32)

    h = h_ref[...]
    a = jnp.dot(h, wg_ref[0], preferred_element_type=F32)
    b = jnp.dot(h, wu_ref[0], preferred_element_type=F32)
    mid = (_silu(a) * b).astype(BF16)
    out = jnp.dot(mid, wd_ref[0], preferred_element_type=F32)
    lane = lax.broadcasted_iota(I32, comb_ref.shape, 1)
    w = jnp.sum(jnp.where(lane == e + N_GROUPS, comb_ref[...], 0.0), axis=-1, keepdims=True)
    acc_scr[...] += w * out

    @pl.when(e == N_EXPERTS - 1)
    def _():
        y = x1_ref[...].reshape(bb, tt, D) + gate_ref[...] * acc_scr[...].reshape(bb, tt, D)
        y_ref[...] = y.reshape(bb * tt, D)


def _moe(h2, comb, x1, gate2, wg, wu, wd, *, bb, tt):
    N, D = x1.shape
    tm = bb * tt
    tok = lambda w: pl.BlockSpec((tm, w), lambda i, e: (i, 0))
    mod = pl.BlockSpec((bb, 1, D), lambda i, e: (i if bb > 1 else 0, 0, 0))
    return pl.pallas_call(
        functools.partial(_moe_kernel, bb=bb, tt=tt),
        grid=(N // tm, N_EXPERTS),
        in_specs=[tok(D), tok(LANES), tok(D), mod,
                  pl.BlockSpec((1, D, D_EXPERT), lambda i, e: (e, 0, 0)),
                  pl.BlockSpec((1, D, D_EXPERT), lambda i, e: (e, 0, 0)),
                  pl.BlockSpec((1, D_EXPERT, D), lambda i, e: (e, 0, 0))],
        out_specs=tok(D),
        out_shape=jax.ShapeDtypeStruct((N, D), F32),
        scratch_shapes=[pltpu.VMEM((tm, D), F32)],
        compiler_params=_params(("parallel", "arbitrary")),
        name="moe",
    )(h2, comb, x1, gate2, wg, wu, wd)


def _layer(x, mod, past, W, *, bb, tt, TQ, TK, KC, ret_chunk):
    B, T, D = x.shape
    N = B * T
    x2d = x.reshape(N, D)
    shift1, scale1, gate1, shift2, scale2, gate2 = [m.reshape(B, 1, D) for m in jnp.split(mod, 6, axis=-1)]
    z = _inproj(x2d, shift1, scale1, W["g1"], W["w_in"], W["qg"], W["kg"], bb=bb, tt=tt)
    vT_new = z["avT"].reshape(N_HEADS, VT_ROWS, B, T).transpose(2, 0, 1, 3)

    if past is None:
        state0 = jnp.zeros((B, N_HEADS, HEAD_DIM, HEAD_DIM), F32)
        P, Lreal = 0, T
        topk = min(TOPK_MAX, T // 4)
        kb = z["akb"].reshape(B, T, WIDTH)
        vT = vT_new
        ikb = z["ikb"].reshape(B, T, IDX_DIM)
        Tq = T
        pad_q = lambda a: a.reshape(B, T, a.shape[-1])
    else:
        ck, cv, cki, state0 = past
        P = ck.shape[1]
        Lreal = P + T
        topk = min(TOPK_MAX, Lreal // 4)
        Lp = -(-Lreal // TK) * TK
        cat = lambda c, n, w: jnp.concatenate(
            [c.reshape(B, P, w).astype(BF16), n.reshape(B, T, w), jnp.zeros((B, Lp - Lreal, w), BF16)], axis=1)
        kb = cat(ck, z["akb"], WIDTH)
        ikb = cat(cki, z["ikb"], IDX_DIM)
        vT_past = jnp.concatenate([cv.astype(BF16).transpose(0, 2, 3, 1),
                                   jnp.ones((B, N_HEADS, VT_ROWS - HEAD_DIM, P), BF16)], axis=2)
        vT = jnp.concatenate([vT_past, vT_new, jnp.zeros((B, N_HEADS, VT_ROWS, Lp - Lreal), BF16)], axis=3)
        Tq = -(-T // TQ) * TQ
        pad_q = lambda a: jnp.pad(a.reshape(B, T, a.shape[-1]), ((0, 0), (0, Tq - T), (0, 0)))

    iq = pad_q(z["iq"])
    iwT = pad_q(z["iw"]).reshape(B, Tq // TQ, TQ, IDX_HEADS).transpose(0, 1, 3, 2)
    keysT, tn = _index_select(iq, ikb, iwT, TQ=TQ, TK=TK, P=P, Lreal=Lreal, topk=topk)
    attn = _attention(pad_q(z["aq"]), kb, vT, keysT, tn, TQ=TQ, TK=TK, KC=KC, P=P, Lreal=Lreal)
    attn = attn[:, :T].reshape(N, WIDTH)

    ret, ret_state = _retention(z["rq"], z["rk"], z["rv"], z["rg"], state0, B=B, T=T, C=ret_chunk)
    x1, h2, comb = _outproj(attn, ret, x2d, gate1, shift2, scale2, W["g2"], W["w_out"], W["w_r"], bb=bb, tt=tt)
    y = _moe(h2, comb, x1, gate2, W["wg"], W["wu"], W["wd"], bb=bb, tt=tt)
    return (y.reshape(B, T, D), z["ak"].reshape(B, T, N_HEADS, HEAD_DIM),
            z["av"].reshape(B, T, N_HEADS, HEAD_DIM), z["ik"].reshape(B, T, IDX_DIM), ret_state)


def kernel(x_prompt, x_sample, cache_k, cache_v, cache_kidx, state_ret, c_prompt, c_sample, w_ada, b_ada,
           norm1_g, norm2_g, w_in, q_norm_g, k_norm_g, w_out, w_group, w_router, w_gate_e, w_up_e, w_down_e):
    depth = w_ada.shape[0]
    D = D_MODEL
    Bp, Bs = x_prompt.shape[0], x_sample.shape[0]
    Ts = x_sample.shape[1]
    xp, xs = x_prompt, x_sample
    outs_p, outs_s = [], []
    for l in range(depth):
        pad_cols = N_COL_GROUPS * WIDTH - w_in.shape[-1]
        w_r = jnp.concatenate(
            [w_group[l], jnp.moveaxis(w_router[l], 0, 1).reshape(D, N_EXPERTS),
             jnp.zeros((D, LANES - N_GROUPS - N_EXPERTS), F32)], axis=1)
        W = {
            "g1": norm1_g[l].reshape(1, D), "g2": norm2_g[l].reshape(1, D),
            "qg": q_norm_g[l].reshape(1, HEAD_DIM), "kg": k_norm_g[l].reshape(1, HEAD_DIM),
            "w_in": jnp.concatenate([w_in[l][:, :OFF_RQ], jnp.zeros((D, pad_cols), F32),
                                     w_in[l][:, OFF_RQ:]], axis=1).astype(BF16),
            "w_out": w_out[l].astype(BF16), "w_r": w_r,
            "wg": w_gate_e[l].astype(BF16), "wu": w_up_e[l].astype(BF16), "wd": w_down_e[l].astype(BF16),
        }
        rows = Bp + Bs
        rows_p = -(-rows // 8) * 8
        c_all = jnp.concatenate([c_prompt, c_sample, jnp.zeros((rows_p - rows, D), F32)], axis=0)
        mod = _adaln_mod(c_all, w_ada[l], b_ada[l])
        past = (cache_k[l], cache_v[l], cache_kidx[l], state_ret[l])
        xp, kp, vp, kip, sp = _layer(xp, mod[:Bp], None, W, bb=1, tt=512, TQ=256, TK=512, KC=256,
                                     ret_chunk=256)
        xs, kn, vn, kin, sn = _layer(xs, mod[Bp:rows], past, W, bb=512 // Ts, tt=Ts, TQ=128, TK=384, KC=384,
                                     ret_chunk=Ts)
        outs_p.append((kp, vp, kip, sp))
        outs_s.append((kn, vn, kin, sn))
    st = lambda xs_, i: jnp.stack([o[i] for o in xs_])
    return (xp, xs, st(outs_p, 0), st(outs_p, 1), st(outs_p, 2), st(outs_p, 3),
            st(outs_s, 0), st(outs_s, 1), st(outs_s, 2), st(outs_s, 3))
```

```python
import functools
import math

import numpy as np
import jax
import jax.numpy as jnp
from jax import lax
from jax.experimental import pallas as pl
from jax.experimental.pallas import tpu as pltpu

F32 = jnp.float32
BF16 = jnp.bfloat16
I32 = jnp.int32

D_MODEL = 2048
CHUNK = 64
N_HEADS = 8
HEAD_DIM = 128
WIDTH = N_HEADS * HEAD_DIM
IDX_HEADS = 8
IDX_DIM = 64
TOPK_MAX = 256
ATTN_SCALE = HEAD_DIM ** -0.5
IDX_SCALE = IDX_DIM ** -0.5
IDX_W_SCALE = IDX_HEADS ** -0.5
RET_K_SCALE = HEAD_DIM ** -0.5
N_GROUPS = 4
EXPERTS_PER_GROUP = 4
N_EXPERTS = N_GROUPS * EXPERTS_PER_GROUP
D_EXPERT = 512
EPS = 1e-6

OFF_IQ = 3 * WIDTH
IDX_COLS = IDX_HEADS * IDX_DIM + IDX_DIM + IDX_HEADS
OFF_RQ = OFF_IQ + IDX_COLS
N_COL_GROUPS = 8

LANES = 128
INT_MIN = -(2 ** 31)
TAKE_ALL = 2 ** 30
MASKED_DIST = 1e33
VMEM_LIMIT = 56 * 1024 * 1024
MOE_TOKENS = 1024
MOE_ROWS = 160
MOE_SCATTER_K = 256

LOG2E = 1.4426950408889634
VT_ROWS = HEAD_DIM + 16
ALIBI_SLOPES = [float(2.0 ** (-8.0 * (h + 1) / N_HEADS)) for h in range(N_HEADS)]
LOG_GAMMA = [float(np.log1p(-(2.0 ** (-5.0 - h)))) for h in range(N_HEADS)]


def _params(sem):
    return pltpu.CompilerParams(dimension_semantics=sem, vmem_limit_bytes=VMEM_LIMIT)


def _silu(x):
    return x * jax.nn.sigmoid(x)


def _mod_kernel(c_ref, w_ref, b_ref, o_ref):
    s = _silu(c_ref[...])
    o_ref[...] = jnp.dot(s, w_ref[...], precision=lax.Precision.HIGHEST,
                         preferred_element_type=F32) + b_ref[...]


def _adaln_mod(c, w_ada, b_ada):
    R, D = c.shape
    N = w_ada.shape[1]
    tn = 1024
    return pl.pallas_call(
        _mod_kernel,
        grid=(N // tn,),
        in_specs=[pl.BlockSpec((R, D), lambda j: (0, 0)),
                  pl.BlockSpec((D, tn), lambda j: (0, j)),
                  pl.BlockSpec((1, tn), lambda j: (0, j))],
        out_specs=pl.BlockSpec((R, tn), lambda j: (0, j)),
        out_shape=jax.ShapeDtypeStruct((R, N), F32),
        compiler_params=_params(("arbitrary",)),
        name="adaln_mod",
    )(c, w_ada, b_ada.reshape(1, N))


def _inproj_kernel(x_ref, shift_ref, scale_ref, g1_ref, w_ref, qg_ref, kg_ref,
                   aq_ref, ak_ref, akb_ref, av_ref, avT_ref, iq_ref, ik_ref, ikb_ref, iw_ref,
                   rq_ref, rk_ref, rv_ref, rg_ref, h_scr, *, bb, tt):
    j = pl.program_id(1)
    D = x_ref.shape[-1]

    @pl.when(j == 0)
    def _():
        x = x_ref[...]
        ms = jnp.mean(x * x, axis=-1, keepdims=True)
        y = x * lax.rsqrt(ms + EPS) * g1_ref[...]
        y = y.reshape(bb, tt, D) * (1.0 + scale_ref[...]) + shift_ref[...]
        h_scr[...] = y.reshape(bb * tt, D).astype(BF16)

    CW = 2 * HEAD_DIM
    chunks = [slice(c * CW, (c + 1) * CW) for c in range(WIDTH // CW)]

    def proj(cols):
        return jnp.dot(h_scr[...], w_ref[:, cols], preferred_element_type=F32)

    def head_rms(zc, g_ref):
        outs = []
        for hh in range(CW // HEAD_DIM):
            zh = zc[:, hh * HEAD_DIM:(hh + 1) * HEAD_DIM]
            ms = jnp.mean(zh * zh, axis=-1, keepdims=True)
            outs.append(zh * lax.rsqrt(ms + EPS) * g_ref[...])
        return jnp.concatenate(outs, axis=1)

    @pl.when(j == 0)
    def _():
        for cols in chunks:
            v = head_rms(proj(cols), qg_ref)
            aq_ref[:, cols] = (v * (ATTN_SCALE * LOG2E)).astype(BF16)

    @pl.when(j == 1)
    def _():
        for cols in chunks:
            v = head_rms(proj(cols), kg_ref)
            ak_ref[:, cols] = v
            akb_ref[:, cols] = v.astype(BF16)

    @pl.when(j == 2)
    def _():
        for c, cols in enumerate(chunks):
            zc = proj(cols)
            av_ref[:, cols] = zc
            zT = zc.T
            for hh in range(CW // HEAD_DIM):
                h = c * (CW // HEAD_DIM) + hh
                avT_ref[h, 0:HEAD_DIM, :] = zT[hh * HEAD_DIM:(hh + 1) * HEAD_DIM, :].astype(BF16)
                avT_ref[h, HEAD_DIM:VT_ROWS, :] = jnp.ones((VT_ROWS - HEAD_DIM, zT.shape[1]), BF16)

    @pl.when(j == 3)
    def _():
        nq = IDX_HEADS * IDX_DIM
        for cols in chunks[:nq // CW]:
            iq_ref[:, cols] = (proj(cols) * IDX_SCALE).astype(BF16)
        zc = proj(chunks[nq // CW])
        ik = zc[:, 0:IDX_DIM]
        ik_ref[...] = ik
        ikb_ref[...] = ik.astype(BF16)
        iw_ref[...] = zc[:, IDX_DIM:IDX_DIM + IDX_HEADS] * IDX_W_SCALE

    @pl.when(j == 4)
    def _():
        for cols in chunks:
            rq_ref[:, cols] = proj(cols).astype(BF16)

    @pl.when(j == 5)
    def _():
        for cols in chunks:
            rk_ref[:, cols] = (proj(cols) * RET_K_SCALE).astype(BF16)

    @pl.when(j == 6)
    def _():
        for cols in chunks:
            rv_ref[:, cols] = proj(cols).astype(BF16)

    @pl.when(j == 7)
    def _():
        for cols in chunks:
            rg_ref[:, cols] = proj(cols)


def _inproj(x2d, shift, scale, g1, w_p, qg, kg, *, bb, tt):
    N, D = x2d.shape
    tm = bb * tt
    nI = N // tm
    tok = lambda w: pl.BlockSpec((tm, w), lambda i, j: (i, 0))
    mod = pl.BlockSpec((bb, 1, D), lambda i, j: (i if bb > 1 else 0, 0, 0))
    row = lambda w: pl.BlockSpec((1, w), lambda i, j: (0, 0))
    outs = [("aq", WIDTH, BF16), ("ak", WIDTH, F32), ("akb", WIDTH, BF16), ("av", WIDTH, F32),
            ("avT", None, BF16), ("iq", IDX_HEADS * IDX_DIM, BF16), ("ik", IDX_DIM, F32),
            ("ikb", IDX_DIM, BF16), ("iw", IDX_HEADS, F32), ("rq", WIDTH, BF16), ("rk", WIDTH, BF16),
            ("rv", WIDTH, BF16), ("rg", WIDTH, F32)]
    out_specs = [tok(w) if w else pl.BlockSpec((N_HEADS, VT_ROWS, tm), lambda i, j: (0, 0, i))
                 for _, w, _ in outs]
    out_shape = [jax.ShapeDtypeStruct((N, w) if w else (N_HEADS, VT_ROWS, N), dt) for _, w, dt in outs]
    res = pl.pallas_call(
        functools.partial(_inproj_kernel, bb=bb, tt=tt),
        grid=(nI, N_COL_GROUPS),
        in_specs=[tok(D), mod, mod, row(D),
                  pl.BlockSpec((D, WIDTH), lambda i, j: (0, j)), row(HEAD_DIM), row(HEAD_DIM)],
        out_specs=out_specs,
        out_shape=out_shape,
        scratch_shapes=[pltpu.VMEM((tm, D), BF16)],
        compiler_params=_params(("parallel", "arbitrary")),
        name="inproj",
    )(x2d, shift, scale, g1, w_p, qg, kg)
    return {name: r for (name, _, _), r in zip(outs, res)}


def _ret_kernel(q_ref, k_ref, v_ref, g_ref, s0_ref, o_ref, sn_ref, st_scr, *, C):
    c = pl.program_id(1)

    @pl.when(c == 0)
    def _():
        st_scr[...] = s0_ref[0]

    pi = lax.broadcasted_iota(I32, (C, C), 0)
    pj = lax.broadcasted_iota(I32, (C, C), 1)
    diff = (pi - pj).astype(F32)
    causal = pi >= pj
    pos = lax.broadcasted_iota(I32, (C, HEAD_DIM), 0).astype(F32)
    for h in range(N_HEADS):
        lg = LOG_GAMMA[h]
        sl = slice(h * HEAD_DIM, (h + 1) * HEAD_DIM)
        q = q_ref[:, sl]
        k = k_ref[:, sl]
        v = v_ref[:, sl]
        decay = jnp.where(causal, jnp.exp(lg * jnp.maximum(diff, 0.0)), 0.0)
        s = lax.dot_general(q, k, (((1,), (1,)), ((), ())), preferred_element_type=F32) * decay
        o = jnp.dot(s.astype(BF16), v, preferred_element_type=F32)
        st = st_scr[h]
        cross = jnp.exp(lg * (pos + 1.0))
        o = o + jnp.dot(q, st.astype(BF16), preferred_element_type=F32) * cross
        kdec = jnp.exp(lg * (C - 1.0 - pos))
        kd = (k.astype(F32) * kdec).astype(BF16)
        st_new = math.exp(lg * C) * st + lax.dot_general(
            kd, v, (((0,), (0,)), ((), ())), preferred_element_type=F32)
        st_scr[h] = st_new
        ms = jnp.mean(o * o, axis=-1, keepdims=True)
        o_ref[:, sl] = (o * lax.rsqrt(ms + EPS) * _silu(g_ref[:, sl])).astype(BF16)

    @pl.when(c == pl.num_programs(1) - 1)
    def _():
        sn_ref[0] = st_scr[...]


def _retention(rq, rk, rv, rg, state0, *, B, T, C):
    nC = T // C
    tok = pl.BlockSpec((C, WIDTH), lambda b, c: (b * nC + c, 0))
    st = pl.BlockSpec((1, N_HEADS, HEAD_DIM, HEAD_DIM), lambda b, c: (b, 0, 0, 0))
    return pl.pallas_call(
        functools.partial(_ret_kernel, C=C),
        grid=(B, nC),
        in_specs=[tok, tok, tok, tok, st],
        out_specs=[tok, st],
        out_shape=[jax.ShapeDtypeStruct((B * T, WIDTH), BF16),
                   jax.ShapeDtypeStruct((B, N_HEADS, HEAD_DIM, HEAD_DIM), F32)],
        scratch_shapes=[pltpu.VMEM((N_HEADS, HEAD_DIM, HEAD_DIM), F32)],
        compiler_params=_params(("parallel", "arbitrary")),
        name="retention",
    )(rq, rk, rv, rg, state0)


def _key_limit(q_first, q_count, P, Lreal):
    return jnp.minimum(Lreal, ((P + q_first + q_count - 1) // CHUNK + 1) * CHUNK)


def _idx_kernel(iq_ref, ik_ref, iwT_ref, keys_ref, tn_ref, *, TQ, TK, P, Lreal, topk):
    qb = pl.program_id(1)
    nkt = (_key_limit(qb * TQ, TQ, P, Lreal) + TK - 1) // TK
    tq = P + qb * TQ + lax.broadcasted_iota(I32, (TK, TQ), 1)
    lrow = lax.broadcasted_iota(I32, (TK, TQ), 0)

    def tile_body(kt, carry):
        r0 = pl.multiple_of(kt * TK, TK)
        ik = ik_ref[0, pl.ds(r0, TK), :]
        acc = jnp.zeros((TK, TQ), F32)
        for h in range(IDX_HEADS):
            qh = iq_ref[0, :, h * IDX_DIM:(h + 1) * IDX_DIM]
            r = lax.dot_general(ik, qh, (((1,), (1,)), ((), ())), preferred_element_type=F32)
            acc = acc + jnp.maximum(r, 0.0) * iwT_ref[0, 0, h:h + 1, :]
        l = r0 + lrow
        allowed = (l < Lreal) & ((l >> 6) <= (tq >> 6))
        bits = pltpu.bitcast(acc, I32)
        key = bits ^ ((bits >> 31) & 0x7FFFFFFF)
        keys_ref[0, 0, pl.ds(r0, TK), :] = jnp.where(allowed, key, INT_MIN)
        return carry

    lax.fori_loop(0, nkt, tile_body, 0)

    def fill_body(kt, carry):
        keys_ref[0, 0, pl.ds(pl.multiple_of(kt * TK, TK), TK), :] = jnp.full((TK, TQ), INT_MIN, I32)
        return carry

    lax.fori_loop(nkt, keys_ref.shape[2] // TK, fill_body, 0)

    SUB = 64

    def count(pred):
        def body(kt, acc):
            r0 = pl.multiple_of(kt * TK, TK)
            for s in range(TK // SUB):
                blk = keys_ref[0, 0, pl.ds(r0 + s * SUB, SUB), :]
                acc = acc + pred(blk).astype(I32)
            return acc
        acc = lax.fori_loop(0, nkt, body, jnp.zeros((SUB, TQ), I32))
        return jnp.sum(acc, axis=0, keepdims=True)

    def pass_body(b, carry):
        t_u, cnt_t = carry
        cand_u = t_u | lax.shift_left(jnp.int32(1), jnp.asarray(31 - b, I32))
        cand_s = cand_u ^ INT_MIN
        cnt = count(lambda blk: blk >= cand_s)
        take = cnt >= topk
        return jnp.where(take, cand_u, t_u), jnp.where(take, cnt, cnt_t)

    t_u, cnt_t = lax.fori_loop(0, 32, pass_body,
                               (jnp.zeros((1, TQ), I32), jnp.full((1, TQ), TAKE_ALL, I32)))
    t_s = t_u ^ INT_MIN
    ties = (cnt_t > topk) & (t_s != INT_MIN)
    any_ties = jnp.max(ties.astype(I32)) > 0
    cnt_gt = lax.cond(any_ties, lambda: count(lambda blk: blk > t_s), lambda: jnp.zeros((1, TQ), I32))
    n_take = jnp.where(ties, topk - cnt_gt, TAKE_ALL)
    row = lax.broadcasted_iota(I32, (8, TQ), 0)
    tn_ref[0, 0] = jnp.where(row == 0, t_s, n_take)


def _index_select(iq, ikb, iwT, *, TQ, TK, P, Lreal, topk):
    nB, Lp, _ = ikb.shape
    nQ = iwT.shape[1]
    return pl.pallas_call(
        functools.partial(_idx_kernel, TQ=TQ, TK=TK, P=P, Lreal=Lreal, topk=topk),
        grid=(nB, nQ),
        in_specs=[pl.BlockSpec((1, TQ, IDX_HEADS * IDX_DIM), lambda b, q: (b, q, 0)),
                  pl.BlockSpec((1, Lp, IDX_DIM), lambda b, q: (b, 0, 0)),
                  pl.BlockSpec((1, 1, IDX_HEADS, TQ), lambda b, q: (b, q, 0, 0))],
        out_specs=[pl.BlockSpec((1, 1, Lp, TQ), lambda b, q: (b, q, 0, 0)),
                   pl.BlockSpec((1, 1, 8, TQ), lambda b, q: (b, q, 0, 0))],
        out_shape=[jax.ShapeDtypeStruct((nB, nQ, Lp, TQ), I32),
                   jax.ShapeDtypeStruct((nB, nQ, 8, TQ), I32)],
        compiler_params=_params(("parallel", "arbitrary")),
        name="index_select",
    )(iq, ikb, iwT)


def _attn_kernel(qbs_ref, kts_ref, last_ref, q_ref, k_ref, vT_ref, keys_ref, tn_ref, o_ref,
                 qT_scr, ndm_scr, a_scr, m_scr, l_scr, acc_scr, tie_scr, *, TQ, TK, KC, P, Lreal):
    s_id = pl.program_id(1)
    qb = qbs_ref[s_id]
    kt = kts_ref[s_id]
    is_last = last_ref[s_id] == 1
    heads = [slice(h * HEAD_DIM, (h + 1) * HEAD_DIM) for h in range(N_HEADS)]

    @pl.when(kt == 0)
    def _():
        m_scr[...] = jnp.full(m_scr.shape, -jnp.inf, F32)
        l_scr[...] = jnp.zeros(l_scr.shape, F32)
        acc_scr[...] = jnp.zeros(acc_scr.shape, F32)
        tie_scr[...] = jnp.zeros(tie_scr.shape, F32)
        for sl in heads:
            qT_scr[sl, :] = q_ref[0, :, sl].astype(F32).T.astype(BF16)

    keys = keys_ref[0, 0]
    thr = tn_ref[0, 0, 0:1, :]
    ntk = tn_ref[0, 0, 1:2, :]
    l = kt * TK + lax.broadcasted_iota(I32, (TK, TQ), 0)
    tq = P + qb * TQ + lax.broadcasted_iota(I32, (TK, TQ), 1)
    has_ties = jnp.max(jnp.where(ntk == TAKE_ALL, 0, 1)) > 0
    no_ties = jnp.logical_not(has_ties)

    @pl.when(jnp.logical_and(no_ties, jnp.logical_not(is_last)))
    def _():
        ndm_scr[...] = jnp.where(keys >= thr, (l - tq).astype(F32), -MASKED_DIST)

    @pl.when(jnp.logical_and(no_ties, is_last))
    def _():
        allowed = (l < Lreal) & ((l >> 6) <= (tq >> 6))
        ndm_scr[...] = jnp.where((keys >= thr) & allowed, -jnp.abs(tq - l).astype(F32), -MASKED_DIST)

    @pl.when(has_ties)
    def _():
        allowed = (l < Lreal) & ((l >> 6) <= (tq >> 6))
        eq = keys == thr
        li = lax.broadcasted_iota(I32, (TK, TK), 0)
        lj = lax.broadcasted_iota(I32, (TK, TK), 1)
        lower = jnp.where(lj <= li, 1.0, 0.0).astype(BF16)
        prefix = jnp.dot(lower, jnp.where(eq, 1.0, 0.0).astype(BF16), preferred_element_type=F32)
        rank = tie_scr[0:1, :] + prefix
        sel = ((keys > thr) | (eq & (rank <= ntk.astype(F32)))) & allowed
        ndm_scr[...] = jnp.where(sel, -jnp.abs(tq - l).astype(F32), -MASKED_DIST)
        tie_scr[0:1, :] = tie_scr[0:1, :] + prefix[TK - 1:TK, :]

    chunks = [slice(c * KC, (c + 1) * KC) for c in range(TK // KC)]

    def stage_a(h, rows, mx):
        s = jnp.dot(k_ref[0, rows, heads[h]], qT_scr[heads[h], :], preferred_element_type=F32)
        a = s + (ALIBI_SLOPES[h] * LOG2E) * ndm_scr[rows, :]
        a_scr[h % 2, rows, :] = a
        cm = jnp.max(a, axis=0, keepdims=True)
        return cm if mx is None else jnp.maximum(mx, cm)

    def stage_b(h, rows, m_new, pv):
        p = jnp.exp2(a_scr[h % 2, rows, :] - m_new).astype(BF16)
        d = jnp.dot(vT_ref[0, h, :, rows], p, preferred_element_type=F32)
        return d if pv is None else pv + d

    m_all = m_scr[...]
    l_all = l_scr[...]
    m_out, l_out = [], []
    mx = None
    for rows in chunks:
        mx = stage_a(0, rows, mx)
    for h in range(N_HEADS):
        m_prev = m_all[h:h + 1]
        m_new = jnp.maximum(m_prev, mx)
        alpha = jnp.exp2(m_prev - m_new)
        mx, pv = None, None
        for rows in chunks:
            if h + 1 < N_HEADS:
                mx = stage_a(h + 1, rows, mx)
            pv = stage_b(h, rows, m_new, pv)
        acc_scr[heads[h], :] = alpha * acc_scr[heads[h], :] + pv[0:HEAD_DIM]
        l_out.append(alpha * l_all[h:h + 1] + pv[HEAD_DIM:HEAD_DIM + 1])
        m_out.append(m_new)
    m_scr[...] = jnp.concatenate(m_out, axis=0)
    l_scr[...] = jnp.concatenate(l_out, axis=0)

    @pl.when(is_last)
    def _():
        for h in range(N_HEADS):
            o_ref[0, :, heads[h]] = (acc_scr[heads[h], :] / l_scr[h:h + 1, :]).T.astype(BF16)


def _attention(aq, kb, vT, keysT, tn, *, TQ, TK, KC, P, Lreal):
    nB, Tq, _ = aq.shape
    nQ = Tq // TQ
    qbs, kts, last = [], [], []
    for qb in range(nQ):
        limit = min(Lreal, ((P + qb * TQ + TQ - 1) // CHUNK + 1) * CHUNK)
        n = -(-limit // TK)
        assert (n - 1) * TK <= P + qb * TQ
        qbs += [qb] * n
        kts += list(range(n))
        last += [0] * (n - 1) + [1]
    steps = len(qbs)
    qmap = lambda b, s, qbs, kts, last: (b, qbs[s], 0)
    grid_spec = pltpu.PrefetchScalarGridSpec(
        num_scalar_prefetch=3,
        grid=(nB, steps),
        in_specs=[pl.BlockSpec((1, TQ, WIDTH), qmap),
                  pl.BlockSpec((1, TK, WIDTH), lambda b, s, qbs, kts, last: (b, kts[s], 0)),
                  pl.BlockSpec((1, N_HEADS, VT_ROWS, TK), lambda b, s, qbs, kts, last: (b, 0, 0, kts[s])),
                  pl.BlockSpec((1, 1, TK, TQ), lambda b, s, qbs, kts, last: (b, qbs[s], kts[s], 0)),
                  pl.BlockSpec((1, 1, 8, TQ), lambda b, s, qbs, kts, last: (b, qbs[s], 0, 0))],
        out_specs=pl.BlockSpec((1, TQ, WIDTH), qmap),
        scratch_shapes=[pltpu.VMEM((WIDTH, TQ), BF16),
                        pltpu.VMEM((TK, TQ), F32),
                        pltpu.VMEM((2, TK, TQ), F32),
                        pltpu.VMEM((N_HEADS, TQ), F32),
                        pltpu.VMEM((N_HEADS, TQ), F32),
                        pltpu.VMEM((WIDTH, TQ), F32),
                        pltpu.VMEM((8, TQ), F32)])
    arr = lambda v: jnp.asarray(np.array(v, np.int32))
    return pl.pallas_call(
        functools.partial(_attn_kernel, TQ=TQ, TK=TK, KC=KC, P=P, Lreal=Lreal),
        grid_spec=grid_spec,
        out_shape=jax.ShapeDtypeStruct((nB, Tq, WIDTH), BF16),
        compiler_params=_params(("parallel", "arbitrary")),
        name="sparse_attention",
    )(arr(qbs), arr(kts), arr(last), aq, kb, vT, keysT, tn)


def _outproj_kernel(attn_ref, ret_ref, x_ref, gate_ref, shift_ref, scale_ref, g2_ref, wo_ref, wrh_ref, wrl_ref,
                    x1_ref, h2_ref, comb_ref, *, bb, tt):
    D = x_ref.shape[-1]
    tm = bb * tt
    CW = 512
    chunks = [slice(c * CW, (c + 1) * CW) for c in range(D // CW)]

    ssq = jnp.zeros((tm, 1), F32)
    for cols in chunks:
        mix = (jnp.dot(attn_ref[...], wo_ref[0:WIDTH, cols], preferred_element_type=F32)
               + jnp.dot(ret_ref[...], wo_ref[WIDTH:2 * WIDTH, cols], preferred_element_type=F32))
        x1 = x_ref[:, cols].reshape(bb, tt, CW) + gate_ref[:, :, cols] * mix.reshape(bb, tt, CW)
        x1 = x1.reshape(tm, CW)
        x1_ref[:, cols] = x1
        ssq = ssq + jnp.sum(x1 * x1, axis=-1, keepdims=True)
    rinv = lax.rsqrt(ssq * (1.0 / D) + EPS)

    logits = jnp.zeros((tm, LANES), F32)
    for cols in chunks:
        h2 = (x1_ref[:, cols] * rinv * g2_ref[:, cols]).reshape(bb, tt, CW)
        h2 = (h2 * (1.0 + scale_ref[:, :, cols]) + shift_ref[:, :, cols]).reshape(tm, CW)
        h_hi = h2.astype(BF16)
        h2_ref[:, cols] = h_hi
        h_lo = (h2 - h_hi.astype(F32)).astype(BF16)
        logits = (logits + jnp.dot(h_hi, wrh_ref[cols, :], preferred_element_type=F32)
                  + jnp.dot(h_lo, wrh_ref[cols, :], preferred_element_type=F32)
                  + jnp.dot(h_hi, wrl_ref[cols, :], preferred_element_type=F32))
    lane = lax.broadcasted_iota(I32, logits.shape, 1)
    ninf = -jnp.inf
    rmax = lambda v: jnp.max(v, axis=-1, keepdims=True)
    rsum = lambda v: jnp.sum(v, axis=-1, keepdims=True)
    first = lambda m: jnp.min(jnp.where(m, lane, LANES), axis=-1, keepdims=True)
    gl = jnp.where(lane < N_GROUPS, logits, ninf)
    gmax = rmax(gl)
    g_top = 1.0 / rsum(jnp.exp(gl - gmax))
    g_idx = first(gl == gmax)
    emask = (lane >= N_GROUPS) & (lane < N_GROUPS + N_EXPERTS) & (((lane - N_GROUPS) >> 2) == g_idx)
    el = jnp.where(emask, logits, ninf)
    emax = rmax(el)
    esum = rsum(jnp.exp(el - emax))
    i1 = first(el == emax)
    el2 = jnp.where(lane == i1, ninf, el)
    emax2 = rmax(el2)
    i2 = first(el2 == emax2)
    p1 = 1.0 / esum
    p2 = jnp.exp(emax2 - emax) / esum
    den = p1 + p2
    comb_ref[...] = jnp.where(lane == i1, g_top * (p1 / den),
                              jnp.where(lane == i2, g_top * (p2 / den), 0.0))


def _outproj(attn, ret, x2d, gate1, shift2, scale2, g2, wo, wr_hi, wr_lo, *, bb, tt):
    N, D = x2d.shape
    tm = bb * tt
    tok = lambda w: pl.BlockSpec((tm, w), lambda i: (i, 0))
    mod = pl.BlockSpec((bb, 1, D), lambda i: (i if bb > 1 else 0, 0, 0))
    full = lambda a: pl.BlockSpec(a.shape, lambda i: (0,) * a.ndim)
    return pl.pallas_call(
        functools.partial(_outproj_kernel, bb=bb, tt=tt),
        grid=(N // tm,),
        in_specs=[tok(WIDTH), tok(WIDTH), tok(D), mod, mod, mod, full(g2), full(wo), full(wr_hi), full(wr_lo)],
        out_specs=[tok(D), tok(D), tok(LANES)],
        out_shape=[jax.ShapeDtypeStruct((N, D), F32), jax.ShapeDtypeStruct((N, D), BF16),
                   jax.ShapeDtypeStruct((N, LANES), F32)],
        compiler_params=_params(("parallel",)),
        name="outproj_router",
    )(attn, ret, x2d, gate1, shift2, scale2, g2, wo, wr_hi, wr_lo)


def _moe_kernel(h_ref, comb_ref, x1_ref, gate_ref, wg_ref, wu_ref, wd_ref, y_ref, acc_scr, *, bb, tt):
    e = pl.program_id(1)
    D = x1_ref.shape[-1]

    @pl.when(e == 0)
    def _():
        acc_scr[...] = jnp.zeros(acc_scr.shape, F32)

    h = h_ref[...]
    a = jnp.dot(h, wg_ref[0], preferred_element_type=F32)
    b = jnp.dot(h, wu_ref[0], preferred_element_type=F32)
    mid = (_silu(a) * b).astype(BF16)
    out = jnp.dot(mid, wd_ref[0], preferred_element_type=F32)
    lane = lax.broadcasted_iota(I32, comb_ref.shape, 1)
    w = jnp.sum(jnp.where(lane == e + N_GROUPS, comb_ref[...], 0.0), axis=-1, keepdims=True)
    acc_scr[...] += w * out

    @pl.when(e == N_EXPERTS - 1)
    def _():
        y = x1_ref[...].reshape(bb, tt, D) + gate_ref[...] * acc_scr[...].reshape(bb, tt, D)
        y_ref[...] = y.reshape(bb * tt, D)


def _moe(h2, comb, x1, gate2, wg, wu, wd, *, bb, tt):
    N, D = x1.shape
    tm = bb * tt
    tok = lambda w: pl.BlockSpec((tm, w), lambda i, e: (i, 0))
    mod = pl.BlockSpec((bb, 1, D), lambda i, e: (i if bb > 1 else 0, 0, 0))
    return pl.pallas_call(
        functools.partial(_moe_kernel, bb=bb, tt=tt),
        grid=(N // tm, N_EXPERTS),
        in_specs=[tok(D), tok(LANES), tok(D), mod,
                  pl.BlockSpec((1, D, D_EXPERT), lambda i, e: (e, 0, 0)),
                  pl.BlockSpec((1, D, D_EXPERT), lambda i, e: (e, 0, 0)),
                  pl.BlockSpec((1, D_EXPERT, D), lambda i, e: (e, 0, 0))],
        out_specs=tok(D),
        out_shape=jax.ShapeDtypeStruct((N, D), F32),
        scratch_shapes=[pltpu.VMEM((tm, D), F32)],
        compiler_params=_params(("parallel", "arbitrary")),
        name="moe",
    )(h2, comb, x1, gate2, wg, wu, wd)


def _moe_routed_kernel(h_ref, comb_ref, x1_hbm, gate_ref, wg_ref, wu_ref, wd_ref, y_ref,
                       rank_scr, rankT_scr, gT_scr, sem):
    i = pl.program_id(0)
    e = pl.program_id(1)
    tm, D = y_ref.shape
    R, KP = MOE_ROWS, MOE_SCATTER_K

    @pl.when(e == 0)
    def _():
        residual = pltpu.make_async_copy(x1_hbm.at[pl.ds(i * tm, tm), :], y_ref, sem)
        residual.start()
        comb = comb_ref[...]
        used = comb != 0.0
        ti = lax.broadcasted_iota(I32, (tm, tm), 0)
        tj = lax.broadcasted_iota(I32, (tm, tm), 1)
        earlier = jnp.where(tj < ti, 1.0, 0.0).astype(BF16)
        rank = jnp.dot(earlier, jnp.where(used, 1.0, 0.0).astype(BF16), preferred_element_type=F32)
        rank = jnp.where(used, rank, -1.0)
        rank_scr[...] = rank
        rankT_scr[...] = rank.T
        gT_scr[...] = comb.T
        residual.wait()

    lane_e = e + N_GROUPS
    lane = lax.broadcasted_iota(I32, (tm, LANES), 1)
    r_col = jnp.sum(jnp.where(lane == lane_e, rank_scr[...], 0.0), axis=-1, keepdims=True)
    r_row = rankT_scr[pl.ds(lane_e, 1), :]
    g_row = gT_scr[pl.ds(lane_e, 1), :]
    n_e = jnp.sum(jnp.where(r_row >= 0.0, 1, 0))
    gate2 = gate_ref[0]
    CW = 512

    def chunk(c, carry):
        base = jnp.asarray(c * R, F32)
        ridx = base + lax.broadcasted_iota(I32, (R, tm), 0).astype(F32)
        pm = r_row == ridx
        x = jnp.dot(jnp.where(pm, 1.0, 0.0).astype(BF16), h_ref[...], preferred_element_type=F32).astype(BF16)
        a = jnp.dot(x, wg_ref[0], preferred_element_type=F32)
        b = jnp.dot(x, wu_ref[0], preferred_element_type=F32)
        mid = (_silu(a) * b).astype(BF16)
        y = jnp.dot(mid, wd_ref[0], preferred_element_type=F32)
        g_r = jnp.sum(jnp.where(pm, g_row, 0.0), axis=-1, keepdims=True)
        ys = (y * g_r * gate2).astype(BF16)
        ys = jnp.concatenate([ys, jnp.zeros((KP - R, D), BF16)], axis=0)
        cidx = lax.broadcasted_iota(I32, (tm, KP), 1)
        sm = (r_col == base + cidx.astype(F32)) & (cidx < R)
        s = jnp.where(sm, 1.0, 0.0).astype(BF16)
        for c0 in range(0, D, CW):
            y_ref[:, c0:c0 + CW] += jnp.dot(s, ys[:, c0:c0 + CW], preferred_element_type=F32)
        return carry

    lax.fori_loop(0, (n_e + R - 1) // R, chunk, 0)


def _moe_routed(h2, comb, x1, gate2, wg, wu, wd):
    N, D = x1.shape
    tm = MOE_TOKENS
    assert gate2.shape[0] == 1 and N % tm == 0
    tok = lambda w: pl.BlockSpec((tm, w), lambda i, e: (i, 0))
    return pl.pallas_call(
        _moe_routed_kernel,
        grid=(N // tm, N_EXPERTS),
        in_specs=[tok(D), tok(LANES), pl.BlockSpec(memory_space=pl.ANY),
                  pl.BlockSpec((1, 1, D), lambda i, e: (0, 0, 0)),
                  pl.BlockSpec((1, D, D_EXPERT), lambda i, e: (e, 0, 0)),
                  pl.BlockSpec((1, D, D_EXPERT), lambda i, e: (e, 0, 0)),
                  pl.BlockSpec((1, D_EXPERT, D), lambda i, e: (e, 0, 0))],
        out_specs=tok(D),
        out_shape=jax.ShapeDtypeStruct((N, D), F32),
        scratch_shapes=[pltpu.VMEM((tm, LANES), F32), pltpu.VMEM((LANES, tm), F32),
                        pltpu.VMEM((LANES, tm), F32), pltpu.SemaphoreType.DMA(())],
        compiler_params=_params(("parallel", "arbitrary")),
        name="moe_routed",
    )(h2, comb, x1, gate2, wg, wu, wd)


def _layer(x, mod, past, W, *, bb, tt, TQ, TK, KC, ret_chunk):
    B, T, D = x.shape
    N = B * T
    x2d = x.reshape(N, D)
    shift1, scale1, gate1, shift2, scale2, gate2 = [m.reshape(B, 1, D) for m in jnp.split(mod, 6, axis=-1)]
    z = _inproj(x2d, shift1, scale1, W["g1"], W["w_in"], W["qg"], W["kg"], bb=bb, tt=tt)
    vT_new = z["avT"].reshape(N_HEADS, VT_ROWS, B, T)
    vT_new = vT_new.reshape(1, N_HEADS, VT_ROWS, T) if B == 1 else vT_new.transpose(2, 0, 1, 3)

    if past is None:
        state0 = jnp.zeros((B, N_HEADS, HEAD_DIM, HEAD_DIM), F32)
        P, Lreal = 0, T
        topk = min(TOPK_MAX, T // 4)
        kb = z["akb"].reshape(B, T, WIDTH)
        vT = vT_new
        ikb = z["ikb"].reshape(B, T, IDX_DIM)
        Tq = T
        pad_q = lambda a: a.reshape(B, T, a.shape[-1])
    else:
        ck, cv, cki, state0 = past
        P = ck.shape[1]
        Lreal = P + T
        topk = min(TOPK_MAX, Lreal // 4)
        Lp = -(-Lreal // TK) * TK
        cat = lambda c, n, w: jnp.concatenate(
            [c.reshape(B, P, w).astype(BF16), n.reshape(B, T, w), jnp.zeros((B, Lp - Lreal, w), BF16)], axis=1)
        kb = cat(ck, z["akb"], WIDTH)
        ikb = cat(cki, z["ikb"], IDX_DIM)
        vT_past = jnp.concatenate([cv.astype(BF16).transpose(0, 2, 3, 1),
                                   jnp.ones((B, N_HEADS, VT_ROWS - HEAD_DIM, P), BF16)], axis=2)
        vT = jnp.concatenate([vT_past, vT_new, jnp.zeros((B, N_HEADS, VT_ROWS, Lp - Lreal), BF16)], axis=3)
        Tq = -(-T // TQ) * TQ
        pad_q = lambda a: jnp.pad(a.reshape(B, T, a.shape[-1]), ((0, 0), (0, Tq - T), (0, 0)))

    iq = pad_q(z["iq"])
    iwT = pad_q(z["iw"]).reshape(B, Tq // TQ, TQ, IDX_HEADS).transpose(0, 1, 3, 2)
    keysT, tn = _index_select(iq, ikb, iwT, TQ=TQ, TK=TK, P=P, Lreal=Lreal, topk=topk)
    attn = _attention(pad_q(z["aq"]), kb, vT, keysT, tn, TQ=TQ, TK=TK, KC=KC, P=P, Lreal=Lreal)
    attn = attn[:, :T].reshape(N, WIDTH)

    ret, ret_state = _retention(z["rq"], z["rk"], z["rv"], z["rg"], state0, B=B, T=T, C=ret_chunk)
    x1, h2, comb = _outproj(attn, ret, x2d, gate1, shift2, scale2, W["g2"], W["w_out"], W["w_r_hi"],
                            W["w_r_lo"], bb=bb, tt=tt)
    if B == 1 and N % MOE_TOKENS == 0:
        y = _moe_routed(h2, comb, x1, gate2, W["wg"], W["wu"], W["wd"])
    else:
        y = _moe(h2, comb, x1, gate2, W["wg"], W["wu"], W["wd"], bb=bb, tt=tt)
    return (y.reshape(B, T, D), z["ak"].reshape(B, T, N_HEADS, HEAD_DIM),
            z["av"].reshape(B, T, N_HEADS, HEAD_DIM), z["ik"].reshape(B, T, IDX_DIM), ret_state)


def kernel(x_prompt, x_sample, cache_k, cache_v, cache_kidx, state_ret, c_prompt, c_sample, w_ada, b_ada,
           norm1_g, norm2_g, w_in, q_norm_g, k_norm_g, w_out, w_group, w_router, w_gate_e, w_up_e, w_down_e):
    depth = w_ada.shape[0]
    D = D_MODEL
    Bp, Bs = x_prompt.shape[0], x_sample.shape[0]
    Ts = x_sample.shape[1]
    xp, xs = x_prompt, x_sample
    outs_p, outs_s = [], []
    for l in range(depth):
        pad_cols = N_COL_GROUPS * WIDTH - w_in.shape[-1]
        w_r = jnp.concatenate(
            [w_group[l], jnp.moveaxis(w_router[l], 0, 1).reshape(D, N_EXPERTS),
             jnp.zeros((D, LANES - N_GROUPS - N_EXPERTS), F32)], axis=1)
        w_r_hi = w_r.astype(BF16)
        W = {
            "g1": norm1_g[l].reshape(1, D), "g2": norm2_g[l].reshape(1, D),
            "qg": q_norm_g[l].reshape(1, HEAD_DIM), "kg": k_norm_g[l].reshape(1, HEAD_DIM),
            "w_in": jnp.concatenate([w_in[l][:, :OFF_RQ].astype(BF16), jnp.zeros((D, pad_cols), BF16),
                                     w_in[l][:, OFF_RQ:].astype(BF16)], axis=1),
            "w_out": w_out[l].astype(BF16),
            "w_r_hi": w_r_hi, "w_r_lo": (w_r - w_r_hi.astype(F32)).astype(BF16),
            "wg": w_gate_e[l].astype(BF16), "wu": w_up_e[l].astype(BF16), "wd": w_down_e[l].astype(BF16),
        }
        rows = Bp + Bs
        rows_p = -(-rows // 8) * 8
        c_all = jnp.concatenate([c_prompt, c_sample, jnp.zeros((rows_p - rows, D), F32)], axis=0)
        mod = _adaln_mod(c_all, w_ada[l], b_ada[l])
        past = (cache_k[l], cache_v[l], cache_kidx[l], state_ret[l])
        xp, kp, vp, kip, sp = _layer(xp, mod[:Bp], None, W, bb=1, tt=512, TQ=256, TK=512, KC=256,
                                     ret_chunk=256)
        xs, kn, vn, kin, sn = _layer(xs, mod[Bp:rows], past, W, bb=512 // Ts, tt=Ts, TQ=128, TK=384, KC=384,
                                     ret_chunk=Ts)
        outs_p.append((kp, vp, kip, sp))
        outs_s.append((kn, vn, kin, sn))
    st = lambda xs_, i: jnp.stack([o[i] for o in xs_])
    return (xp, xs, st(outs_p, 0), st(outs_p, 1), st(outs_p, 2), st(outs_p, 3),
            st(outs_s, 0), st(outs_s, 1), st(outs_s, 2), st(outs_s, 3))
```

```python
import functools
import math

import numpy as np
import jax
import jax.numpy as jnp
from jax import lax
from jax.experimental import pallas as pl
from jax.experimental.pallas import tpu as pltpu

F32 = jnp.float32
BF16 = jnp.bfloat16
I32 = jnp.int32

D_MODEL = 2048
CHUNK = 64
N_HEADS = 8
HEAD_DIM = 128
WIDTH = N_HEADS * HEAD_DIM
IDX_HEADS = 8
IDX_DIM = 64
TOPK_MAX = 256
ATTN_SCALE = HEAD_DIM ** -0.5
IDX_SCALE = IDX_DIM ** -0.5
IDX_W_SCALE = IDX_HEADS ** -0.5
RET_K_SCALE = HEAD_DIM ** -0.5
N_GROUPS = 4
EXPERTS_PER_GROUP = 4
N_EXPERTS = N_GROUPS * EXPERTS_PER_GROUP
D_EXPERT = 512
EPS = 1e-6

OFF_IQ = 3 * WIDTH
IDX_COLS = IDX_HEADS * IDX_DIM + IDX_DIM + IDX_HEADS
OFF_RQ = OFF_IQ + IDX_COLS
N_COL_GROUPS = 8

LANES = 128
INT_MIN = -(2 ** 31)
TAKE_ALL = 2 ** 30
MASKED_DIST = 1e33
VMEM_LIMIT = 56 * 1024 * 1024
MOE_TOKENS = 1024
MOE_ROWS = 160
MOE_SCATTER_K = 256

LOG2E = 1.4426950408889634
VT_ROWS = HEAD_DIM + 16
ALIBI_SLOPES = [float(2.0 ** (-8.0 * (h + 1) / N_HEADS)) for h in range(N_HEADS)]
LOG_GAMMA = [float(np.log1p(-(2.0 ** (-5.0 - h)))) for h in range(N_HEADS)]


def _params(sem):
    return pltpu.CompilerParams(dimension_semantics=sem, vmem_limit_bytes=VMEM_LIMIT)


def _silu(x):
    return x * jax.nn.sigmoid(x)


def _mod_kernel(c_ref, w_ref, b_ref, o_ref):
    s = _silu(c_ref[...])
    o_ref[...] = jnp.dot(s, w_ref[...], precision=lax.Precision.HIGHEST,
                         preferred_element_type=F32) + b_ref[...]


def _adaln_mod(c, w_ada, b_ada):
    R, D = c.shape
    N = w_ada.shape[1]
    tn = 1024
    return pl.pallas_call(
        _mod_kernel,
        grid=(N // tn,),
        in_specs=[pl.BlockSpec((R, D), lambda j: (0, 0)),
                  pl.BlockSpec((D, tn), lambda j: (0, j)),
                  pl.BlockSpec((1, tn), lambda j: (0, j))],
        out_specs=pl.BlockSpec((R, tn), lambda j: (0, j)),
        out_shape=jax.ShapeDtypeStruct((R, N), F32),
        compiler_params=_params(("arbitrary",)),
        name="adaln_mod",
    )(c, w_ada, b_ada.reshape(1, N))


def _w_in_kernel(w_ref, o_ref):
    rows = w_ref.shape[0]
    pad = o_ref.shape[1] - w_ref.shape[1]
    o_ref[:, 0:OFF_RQ] = w_ref[:, 0:OFF_RQ].astype(BF16)
    o_ref[:, OFF_RQ:OFF_RQ + pad] = jnp.zeros((rows, pad), BF16)
    o_ref[:, OFF_RQ + pad:] = w_ref[:, OFF_RQ:].astype(BF16)


def _pad_cast_w_in(w):
    D, n_in = w.shape
    n_out = N_COL_GROUPS * WIDTH
    rows = 128
    return pl.pallas_call(
        _w_in_kernel,
        grid=(D // rows,),
        in_specs=[pl.BlockSpec((rows, n_in), lambda i: (i, 0))],
        out_specs=pl.BlockSpec((rows, n_out), lambda i: (i, 0)),
        out_shape=jax.ShapeDtypeStruct((D, n_out), BF16),
        compiler_params=_params(("parallel",)),
        name="pad_cast_w_in",
    )(w)


def _inproj_kernel(x_ref, shift_ref, scale_ref, g1_ref, w_ref, qg_ref, kg_ref,
                   aq_ref, ak_ref, akb_ref, av_ref, avT_ref, iq_ref, ik_ref, ikb_ref, iw_ref,
                   rq_ref, rk_ref, rv_ref, rg_ref, h_scr, *, bb, tt):
    j = pl.program_id(1)
    D = x_ref.shape[-1]

    @pl.when(j == 0)
    def _():
        x = x_ref[...]
        ms = jnp.mean(x * x, axis=-1, keepdims=True)
        y = x * lax.rsqrt(ms + EPS) * g1_ref[...]
        y = y.reshape(bb, tt, D) * (1.0 + scale_ref[...]) + shift_ref[...]
        h_scr[...] = y.reshape(bb * tt, D).astype(BF16)

    CW = 2 * HEAD_DIM
    chunks = [slice(c * CW, (c + 1) * CW) for c in range(WIDTH // CW)]

    def proj(cols):
        return jnp.dot(h_scr[...], w_ref[:, cols], preferred_element_type=F32)

    def head_rms(zc, g_ref):
        outs = []
        for hh in range(CW // HEAD_DIM):
            zh = zc[:, hh * HEAD_DIM:(hh + 1) * HEAD_DIM]
            ms = jnp.mean(zh * zh, axis=-1, keepdims=True)
            outs.append(zh * lax.rsqrt(ms + EPS) * g_ref[...])
        return jnp.concatenate(outs, axis=1)

    @pl.when(j == 0)
    def _():
        for cols in chunks:
            v = head_rms(proj(cols), qg_ref)
            aq_ref[:, cols] = (v * (ATTN_SCALE * LOG2E)).astype(BF16)

    @pl.when(j == 1)
    def _():
        for cols in chunks:
            v = head_rms(proj(cols), kg_ref)
            ak_ref[:, cols] = v
            akb_ref[:, cols] = v.astype(BF16)

    @pl.when(j == 2)
    def _():
        for c, cols in enumerate(chunks):
            zc = proj(cols)
            av_ref[:, cols] = zc
            zT = zc.T
            for hh in range(CW // HEAD_DIM):
                h = c * (CW // HEAD_DIM) + hh
                avT_ref[h, 0:HEAD_DIM, :] = zT[hh * HEAD_DIM:(hh + 1) * HEAD_DIM, :].astype(BF16)
                avT_ref[h, HEAD_DIM:VT_ROWS, :] = jnp.ones((VT_ROWS - HEAD_DIM, zT.shape[1]), BF16)

    @pl.when(j == 3)
    def _():
        nq = IDX_HEADS * IDX_DIM
        for cols in chunks[:nq // CW]:
            iq_ref[:, cols] = (proj(cols) * IDX_SCALE).astype(BF16)
        zc = proj(chunks[nq // CW])
        ik = zc[:, 0:IDX_DIM]
        ik_ref[...] = ik
        ikb_ref[...] = ik.astype(BF16)
        iw_ref[...] = zc[:, IDX_DIM:IDX_DIM + IDX_HEADS] * IDX_W_SCALE

    @pl.when(j == 4)
    def _():
        for cols in chunks:
            rq_ref[:, cols] = proj(cols).astype(BF16)

    @pl.when(j == 5)
    def _():
        for cols in chunks:
            rk_ref[:, cols] = (proj(cols) * RET_K_SCALE).astype(BF16)

    @pl.when(j == 6)
    def _():
        for cols in chunks:
            rv_ref[:, cols] = proj(cols).astype(BF16)

    @pl.when(j == 7)
    def _():
        for cols in chunks:
            rg_ref[:, cols] = proj(cols)


def _inproj(x2d, shift, scale, g1, w_p, qg, kg, *, bb, tt):
    N, D = x2d.shape
    tm = bb * tt
    nI = N // tm
    tok = lambda w: pl.BlockSpec((tm, w), lambda i, j: (i, 0))
    mod = pl.BlockSpec((bb, 1, D), lambda i, j: (i if bb > 1 else 0, 0, 0))
    row = lambda w: pl.BlockSpec((1, w), lambda i, j: (0, 0))
    outs = [("aq", WIDTH, BF16), ("ak", WIDTH, F32), ("akb", WIDTH, BF16), ("av", WIDTH, F32),
            ("avT", None, BF16), ("iq", IDX_HEADS * IDX_DIM, BF16), ("ik", IDX_DIM, F32),
            ("ikb", IDX_DIM, BF16), ("iw", IDX_HEADS, F32), ("rq", WIDTH, BF16), ("rk", WIDTH, BF16),
            ("rv", WIDTH, BF16), ("rg", WIDTH, F32)]
    out_specs = [tok(w) if w else pl.BlockSpec((N_HEADS, VT_ROWS, tm), lambda i, j: (0, 0, i))
                 for _, w, _ in outs]
    out_shape = [jax.ShapeDtypeStruct((N, w) if w else (N_HEADS, VT_ROWS, N), dt) for _, w, dt in outs]
    res = pl.pallas_call(
        functools.partial(_inproj_kernel, bb=bb, tt=tt),
        grid=(nI, N_COL_GROUPS),
        in_specs=[tok(D), mod, mod, row(D),
                  pl.BlockSpec((D, WIDTH), lambda i, j: (0, j)), row(HEAD_DIM), row(HEAD_DIM)],
        out_specs=out_specs,
        out_shape=out_shape,
        scratch_shapes=[pltpu.VMEM((tm, D), BF16)],
        compiler_params=_params(("parallel", "arbitrary")),
        name="inproj",
    )(x2d, shift, scale, g1, w_p, qg, kg)
    return {name: r for (name, _, _), r in zip(outs, res)}


def _ret_kernel(q_ref, k_ref, v_ref, g_ref, s0_ref, o_ref, sn_ref, st_scr, *, C):
    c = pl.program_id(1)

    @pl.when(c == 0)
    def _():
        st_scr[...] = s0_ref[0]

    pi = lax.broadcasted_iota(I32, (C, C), 0)
    pj = lax.broadcasted_iota(I32, (C, C), 1)
    diff = (pi - pj).astype(F32)
    causal = pi >= pj
    pos = lax.broadcasted_iota(I32, (C, HEAD_DIM), 0).astype(F32)
    for h in range(N_HEADS):
        lg = LOG_GAMMA[h]
        sl = slice(h * HEAD_DIM, (h + 1) * HEAD_DIM)
        q = q_ref[:, sl]
        k = k_ref[:, sl]
        v = v_ref[:, sl]
        decay = jnp.where(causal, jnp.exp(lg * jnp.maximum(diff, 0.0)), 0.0)
        s = lax.dot_general(q, k, (((1,), (1,)), ((), ())), preferred_element_type=F32) * decay
        o = jnp.dot(s.astype(BF16), v, preferred_element_type=F32)
        st = st_scr[h]
        cross = jnp.exp(lg * (pos + 1.0))
        o = o + jnp.dot(q, st.astype(BF16), preferred_element_type=F32) * cross
        kdec = jnp.exp(lg * (C - 1.0 - pos))
        kd = (k.astype(F32) * kdec).astype(BF16)
        st_new = math.exp(lg * C) * st + lax.dot_general(
            kd, v, (((0,), (0,)), ((), ())), preferred_element_type=F32)
        st_scr[h] = st_new
        ms = jnp.mean(o * o, axis=-1, keepdims=True)
        o_ref[:, sl] = (o * lax.rsqrt(ms + EPS) * _silu(g_ref[:, sl])).astype(BF16)

    @pl.when(c == pl.num_programs(1) - 1)
    def _():
        sn_ref[0] = st_scr[...]


def _retention(rq, rk, rv, rg, state0, *, B, T, C):
    nC = T // C
    tok = pl.BlockSpec((C, WIDTH), lambda b, c: (b * nC + c, 0))
    st = pl.BlockSpec((1, N_HEADS, HEAD_DIM, HEAD_DIM), lambda b, c: (b, 0, 0, 0))
    return pl.pallas_call(
        functools.partial(_ret_kernel, C=C),
        grid=(B, nC),
        in_specs=[tok, tok, tok, tok, st],
        out_specs=[tok, st],
        out_shape=[jax.ShapeDtypeStruct((B * T, WIDTH), BF16),
                   jax.ShapeDtypeStruct((B, N_HEADS, HEAD_DIM, HEAD_DIM), F32)],
        scratch_shapes=[pltpu.VMEM((N_HEADS, HEAD_DIM, HEAD_DIM), F32)],
        compiler_params=_params(("parallel", "arbitrary")),
        name="retention",
    )(rq, rk, rv, rg, state0)


def _cache_kernel(ck_ref, cv_ref, cki_ref, kn_ref, vn_ref, in_ref, kb_ref, vT_ref, ikb_ref):
    P = cki_ref.shape[1]
    T = kn_ref.shape[1]
    Lp = kb_ref.shape[1]
    for h in range(N_HEADS):
        sl = slice(h * HEAD_DIM, (h + 1) * HEAD_DIM)
        kb_ref[0, 0:P, sl] = ck_ref[0, pl.ds(h, P, stride=N_HEADS), :].astype(BF16)
        vT_ref[0, h, 0:HEAD_DIM, 0:P] = cv_ref[0, pl.ds(h, P, stride=N_HEADS), :].T.astype(BF16)
        vT_ref[0, h, HEAD_DIM:VT_ROWS, 0:P] = jnp.ones((VT_ROWS - HEAD_DIM, P), BF16)
        vT_ref[0, h, :, P:Lp] = jnp.concatenate(
            [vn_ref[0, h], jnp.zeros((VT_ROWS, Lp - P - T), BF16)], axis=1)
    kb_ref[0, P:P + T, :] = kn_ref[0]
    kb_ref[0, P + T:Lp, :] = jnp.zeros((Lp - P - T, WIDTH), BF16)
    ikb_ref[0, 0:P, :] = cki_ref[0].astype(BF16)
    ikb_ref[0, P:P + T, :] = in_ref[0]
    ikb_ref[0, P + T:Lp, :] = jnp.zeros((Lp - P - T, IDX_DIM), BF16)


def _assemble_cache(ck, cv, cki, k_new, vT_new, ik_new, Lp):
    B, P = ck.shape[:2]
    T = k_new.shape[1]
    lead = lambda *blk: pl.BlockSpec((1,) + blk, lambda b: (b,) + (0,) * len(blk))
    return pl.pallas_call(
        _cache_kernel,
        grid=(B,),
        in_specs=[lead(P * N_HEADS, HEAD_DIM), lead(P * N_HEADS, HEAD_DIM), lead(P, IDX_DIM),
                  lead(T, WIDTH), lead(N_HEADS, VT_ROWS, T), lead(T, IDX_DIM)],
        out_specs=[lead(Lp, WIDTH), lead(N_HEADS, VT_ROWS, Lp), lead(Lp, IDX_DIM)],
        out_shape=[jax.ShapeDtypeStruct((B, Lp, WIDTH), BF16),
                   jax.ShapeDtypeStruct((B, N_HEADS, VT_ROWS, Lp), BF16),
                   jax.ShapeDtypeStruct((B, Lp, IDX_DIM), BF16)],
        compiler_params=_params(("parallel",)),
        name="assemble_cache",
    )(ck.reshape(B, P * N_HEADS, HEAD_DIM), cv.reshape(B, P * N_HEADS, HEAD_DIM), cki, k_new, vT_new, ik_new)


def _key_limit(q_first, q_count, P, Lreal):
    return jnp.minimum(Lreal, ((P + q_first + q_count - 1) // CHUNK + 1) * CHUNK)


def _idx_kernel(iq_ref, ik_ref, iwT_ref, keys_ref, tn_ref, *, TQ, TK, P, Lreal, topk):
    qb = pl.program_id(1)
    nkt = (_key_limit(qb * TQ, TQ, P, Lreal) + TK - 1) // TK
    tq = P + qb * TQ + lax.broadcasted_iota(I32, (TK, TQ), 1)
    lrow = lax.broadcasted_iota(I32, (TK, TQ), 0)

    def tile_body(kt, carry):
        r0 = pl.multiple_of(kt * TK, TK)
        ik = ik_ref[0, pl.ds(r0, TK), :]
        acc = jnp.zeros((TK, TQ), F32)
        for h in range(IDX_HEADS):
            qh = iq_ref[0, :, h * IDX_DIM:(h + 1) * IDX_DIM]
            r = lax.dot_general(ik, qh, (((1,), (1,)), ((), ())), preferred_element_type=F32)
            acc = acc + jnp.maximum(r, 0.0) * iwT_ref[0, 0, h:h + 1, :]
        l = r0 + lrow
        allowed = (l < Lreal) & ((l >> 6) <= (tq >> 6))
        bits = pltpu.bitcast(acc, I32)
        key = bits ^ ((bits >> 31) & 0x7FFFFFFF)
        keys_ref[0, 0, pl.ds(r0, TK), :] = jnp.where(allowed, key, INT_MIN)
        return carry

    lax.fori_loop(0, nkt, tile_body, 0)

    def fill_body(kt, carry):
        keys_ref[0, 0, pl.ds(pl.multiple_of(kt * TK, TK), TK), :] = jnp.full((TK, TQ), INT_MIN, I32)
        return carry

    lax.fori_loop(nkt, keys_ref.shape[2] // TK, fill_body, 0)

    SUB = 64

    def count(pred):
        def body(kt, acc):
            r0 = pl.multiple_of(kt * TK, TK)
            for s in range(TK // SUB):
                blk = keys_ref[0, 0, pl.ds(r0 + s * SUB, SUB), :]
                acc = acc + pred(blk).astype(I32)
            return acc
        acc = lax.fori_loop(0, nkt, body, jnp.zeros((SUB, TQ), I32))
        return jnp.sum(acc, axis=0, keepdims=True)

    def pass_body(b, carry):
        t_u, cnt_t = carry
        cand_u = t_u | lax.shift_left(jnp.int32(1), jnp.asarray(31 - b, I32))
        cand_s = cand_u ^ INT_MIN
        cnt = count(lambda blk: blk >= cand_s)
        take = cnt >= topk
        return jnp.where(take, cand_u, t_u), jnp.where(take, cnt, cnt_t)

    t_u, cnt_t = lax.fori_loop(0, 32, pass_body,
                               (jnp.zeros((1, TQ), I32), jnp.full((1, TQ), TAKE_ALL, I32)))
    t_s = t_u ^ INT_MIN
    ties = (cnt_t > topk) & (t_s != INT_MIN)
    any_ties = jnp.max(ties.astype(I32)) > 0
    cnt_gt = lax.cond(any_ties, lambda: count(lambda blk: blk > t_s), lambda: jnp.zeros((1, TQ), I32))
    n_take = jnp.where(ties, topk - cnt_gt, TAKE_ALL)
    row = lax.broadcasted_iota(I32, (8, TQ), 0)
    tn_ref[0, 0] = jnp.where(row == 0, t_s, n_take)


def _index_select(iq, ikb, iwT, *, TQ, TK, P, Lreal, topk):
    nB, Lp, _ = ikb.shape
    nQ = iwT.shape[1]
    return pl.pallas_call(
        functools.partial(_idx_kernel, TQ=TQ, TK=TK, P=P, Lreal=Lreal, topk=topk),
        grid=(nB, nQ),
        in_specs=[pl.BlockSpec((1, TQ, IDX_HEADS * IDX_DIM), lambda b, q: (b, q, 0)),
                  pl.BlockSpec((1, Lp, IDX_DIM), lambda b, q: (b, 0, 0)),
                  pl.BlockSpec((1, 1, IDX_HEADS, TQ), lambda b, q: (b, q, 0, 0))],
        out_specs=[pl.BlockSpec((1, 1, Lp, TQ), lambda b, q: (b, q, 0, 0)),
                   pl.BlockSpec((1, 1, 8, TQ), lambda b, q: (b, q, 0, 0))],
        out_shape=[jax.ShapeDtypeStruct((nB, nQ, Lp, TQ), I32),
                   jax.ShapeDtypeStruct((nB, nQ, 8, TQ), I32)],
        compiler_params=_params(("parallel", "arbitrary")),
        name="index_select",
    )(iq, ikb, iwT)


def _attn_kernel(qbs_ref, kts_ref, last_ref, q_ref, k_ref, vT_ref, keys_ref, tn_ref, o_ref,
                 qT_scr, ndm_scr, a_scr, m_scr, l_scr, acc_scr, tie_scr, *, TQ, TK, KC, P, Lreal):
    s_id = pl.program_id(1)
    qb = qbs_ref[s_id]
    kt = kts_ref[s_id]
    is_last = last_ref[s_id] == 1
    heads = [slice(h * HEAD_DIM, (h + 1) * HEAD_DIM) for h in range(N_HEADS)]

    @pl.when(kt == 0)
    def _():
        m_scr[...] = jnp.full(m_scr.shape, -jnp.inf, F32)
        l_scr[...] = jnp.zeros(l_scr.shape, F32)
        acc_scr[...] = jnp.zeros(acc_scr.shape, F32)
        tie_scr[...] = jnp.zeros(tie_scr.shape, F32)
        for sl in heads:
            qT_scr[sl, :] = q_ref[0, :, sl].astype(F32).T.astype(BF16)

    keys = keys_ref[0, 0]
    thr = tn_ref[0, 0, 0:1, :]
    ntk = tn_ref[0, 0, 1:2, :]
    l = kt * TK + lax.broadcasted_iota(I32, (TK, TQ), 0)
    tq = P + qb * TQ + lax.broadcasted_iota(I32, (TK, TQ), 1)
    has_ties = jnp.max(jnp.where(ntk == TAKE_ALL, 0, 1)) > 0
    no_ties = jnp.logical_not(has_ties)

    @pl.when(jnp.logical_and(no_ties, jnp.logical_not(is_last)))
    def _():
        ndm_scr[...] = jnp.where(keys >= thr, (l - tq).astype(F32), -MASKED_DIST)

    @pl.when(jnp.logical_and(no_ties, is_last))
    def _():
        allowed = (l < Lreal) & ((l >> 6) <= (tq >> 6))
        ndm_scr[...] = jnp.where((keys >= thr) & allowed, -jnp.abs(tq - l).astype(F32), -MASKED_DIST)

    @pl.when(has_ties)
    def _():
        allowed = (l < Lreal) & ((l >> 6) <= (tq >> 6))
        eq = keys == thr
        li = lax.broadcasted_iota(I32, (TK, TK), 0)
        lj = lax.broadcasted_iota(I32, (TK, TK), 1)
        lower = jnp.where(lj <= li, 1.0, 0.0).astype(BF16)
        prefix = jnp.dot(lower, jnp.where(eq, 1.0, 0.0).astype(BF16), preferred_element_type=F32)
        rank = tie_scr[0:1, :] + prefix
        sel = ((keys > thr) | (eq & (rank <= ntk.astype(F32)))) & allowed
        ndm_scr[...] = jnp.where(sel, -jnp.abs(tq - l).astype(F32), -MASKED_DIST)
        tie_scr[0:1, :] = tie_scr[0:1, :] + prefix[TK - 1:TK, :]

    chunks = [slice(c * KC, (c + 1) * KC) for c in range(TK // KC)]

    def stage_a(h, rows, mx):
        s = jnp.dot(k_ref[0, rows, heads[h]], qT_scr[heads[h], :], preferred_element_type=F32)
        a = s + (ALIBI_SLOPES[h] * LOG2E) * ndm_scr[rows, :]
        a_scr[h % 2, rows, :] = a
        cm = jnp.max(a, axis=0, keepdims=True)
        return cm if mx is None else jnp.maximum(mx, cm)

    def stage_b(h, rows, m_new, pv):
        p = jnp.exp2(a_scr[h % 2, rows, :] - m_new).astype(BF16)
        d = jnp.dot(vT_ref[0, h, :, rows], p, preferred_element_type=F32)
        return d if pv is None else pv + d

    m_all = m_scr[...]
    l_all = l_scr[...]
    m_out, l_out = [], []
    mx = None
    for rows in chunks:
        mx = stage_a(0, rows, mx)
    for h in range(N_HEADS):
        m_prev = m_all[h:h + 1]
        m_new = jnp.maximum(m_prev, mx)
        alpha = jnp.exp2(m_prev - m_new)
        mx, pv = None, None
        for rows in chunks:
            if h + 1 < N_HEADS:
                mx = stage_a(h + 1, rows, mx)
            pv = stage_b(h, rows, m_new, pv)
        acc_scr[heads[h], :] = alpha * acc_scr[heads[h], :] + pv[0:HEAD_DIM]
        l_out.append(alpha * l_all[h:h + 1] + pv[HEAD_DIM:HEAD_DIM + 1])
        m_out.append(m_new)
    m_scr[...] = jnp.concatenate(m_out, axis=0)
    l_scr[...] = jnp.concatenate(l_out, axis=0)

    @pl.when(is_last)
    def _():
        for h in range(N_HEADS):
            o_ref[0, :, heads[h]] = (acc_scr[heads[h], :] / l_scr[h:h + 1, :]).T.astype(BF16)


def _attention(aq, kb, vT, keysT, tn, *, TQ, TK, KC, P, Lreal):
    nB, Tq, _ = aq.shape
    nQ = Tq // TQ
    qbs, kts, last = [], [], []
    for qb in range(nQ):
        limit = min(Lreal, ((P + qb * TQ + TQ - 1) // CHUNK + 1) * CHUNK)
        n = -(-limit // TK)
        assert (n - 1) * TK <= P + qb * TQ
        qbs += [qb] * n
        kts += list(range(n))
        last += [0] * (n - 1) + [1]
    steps = len(qbs)
    qmap = lambda b, s, qbs, kts, last: (b, qbs[s], 0)
    grid_spec = pltpu.PrefetchScalarGridSpec(
        num_scalar_prefetch=3,
        grid=(nB, steps),
        in_specs=[pl.BlockSpec((1, TQ, WIDTH), qmap),
                  pl.BlockSpec((1, TK, WIDTH), lambda b, s, qbs, kts, last: (b, kts[s], 0)),
                  pl.BlockSpec((1, N_HEADS, VT_ROWS, TK), lambda b, s, qbs, kts, last: (b, 0, 0, kts[s])),
                  pl.BlockSpec((1, 1, TK, TQ), lambda b, s, qbs, kts, last: (b, qbs[s], kts[s], 0)),
                  pl.BlockSpec((1, 1, 8, TQ), lambda b, s, qbs, kts, last: (b, qbs[s], 0, 0))],
        out_specs=pl.BlockSpec((1, TQ, WIDTH), qmap),
        scratch_shapes=[pltpu.VMEM((WIDTH, TQ), BF16),
                        pltpu.VMEM((TK, TQ), F32),
                        pltpu.VMEM((2, TK, TQ), F32),
                        pltpu.VMEM((N_HEADS, TQ), F32),
                        pltpu.VMEM((N_HEADS, TQ), F32),
                        pltpu.VMEM((WIDTH, TQ), F32),
                        pltpu.VMEM((8, TQ), F32)])
    arr = lambda v: jnp.asarray(np.array(v, np.int32))
    return pl.pallas_call(
        functools.partial(_attn_kernel, TQ=TQ, TK=TK, KC=KC, P=P, Lreal=Lreal),
        grid_spec=grid_spec,
        out_shape=jax.ShapeDtypeStruct((nB, Tq, WIDTH), BF16),
        compiler_params=_params(("parallel", "arbitrary")),
        name="sparse_attention",
    )(arr(qbs), arr(kts), arr(last), aq, kb, vT, keysT, tn)


def _outproj_kernel(attn_ref, ret_ref, x_ref, gate_ref, shift_ref, scale_ref, g2_ref, wo_ref, wrh_ref, wrl_ref,
                    x1_ref, h2_ref, comb_ref, *, bb, tt):
    D = x_ref.shape[-1]
    tm = bb * tt
    CW = 512
    chunks = [slice(c * CW, (c + 1) * CW) for c in range(D // CW)]

    ssq = jnp.zeros((tm, 1), F32)
    for cols in chunks:
        mix = (jnp.dot(attn_ref[...], wo_ref[0:WIDTH, cols], preferred_element_type=F32)
               + jnp.dot(ret_ref[...], wo_ref[WIDTH:2 * WIDTH, cols], preferred_element_type=F32))
        x1 = x_ref[:, cols].reshape(bb, tt, CW) + gate_ref[:, :, cols] * mix.reshape(bb, tt, CW)
        x1 = x1.reshape(tm, CW)
        x1_ref[:, cols] = x1
        ssq = ssq + jnp.sum(x1 * x1, axis=-1, keepdims=True)
    rinv = lax.rsqrt(ssq * (1.0 / D) + EPS)

    logits = jnp.zeros((tm, LANES), F32)
    for cols in chunks:
        h2 = (x1_ref[:, cols] * rinv * g2_ref[:, cols]).reshape(bb, tt, CW)
        h2 = (h2 * (1.0 + scale_ref[:, :, cols]) + shift_ref[:, :, cols]).reshape(tm, CW)
        h_hi = h2.astype(BF16)
        h2_ref[:, cols] = h_hi
        h_lo = (h2 - h_hi.astype(F32)).astype(BF16)
        logits = (logits + jnp.dot(h_hi, wrh_ref[cols, :], preferred_element_type=F32)
                  + jnp.dot(h_lo, wrh_ref[cols, :], preferred_element_type=F32)
                  + jnp.dot(h_hi, wrl_ref[cols, :], preferred_element_type=F32))
    lane = lax.broadcasted_iota(I32, logits.shape, 1)
    ninf = -jnp.inf
    rmax = lambda v: jnp.max(v, axis=-1, keepdims=True)
    rsum = lambda v: jnp.sum(v, axis=-1, keepdims=True)
    first = lambda m: jnp.min(jnp.where(m, lane, LANES), axis=-1, keepdims=True)
    gl = jnp.where(lane < N_GROUPS, logits, ninf)
    gmax = rmax(gl)
    g_top = 1.0 / rsum(jnp.exp(gl - gmax))
    g_idx = first(gl == gmax)
    emask = (lane >= N_GROUPS) & (lane < N_GROUPS + N_EXPERTS) & (((lane - N_GROUPS) >> 2) == g_idx)
    el = jnp.where(emask, logits, ninf)
    emax = rmax(el)
    esum = rsum(jnp.exp(el - emax))
    i1 = first(el == emax)
    el2 = jnp.where(lane == i1, ninf, el)
    emax2 = rmax(el2)
    i2 = first(el2 == emax2)
    p1 = 1.0 / esum
    p2 = jnp.exp(emax2 - emax) / esum
    den = p1 + p2
    comb_ref[...] = jnp.where(lane == i1, g_top * (p1 / den),
                              jnp.where(lane == i2, g_top * (p2 / den), 0.0))


def _outproj(attn, ret, x2d, gate1, shift2, scale2, g2, wo, wr_hi, wr_lo, *, bb, tt):
    N, D = x2d.shape
    tm = bb * tt
    tok = lambda w: pl.BlockSpec((tm, w), lambda i: (i, 0))
    mod = pl.BlockSpec((bb, 1, D), lambda i: (i if bb > 1 else 0, 0, 0))
    full = lambda a: pl.BlockSpec(a.shape, lambda i: (0,) * a.ndim)
    return pl.pallas_call(
        functools.partial(_outproj_kernel, bb=bb, tt=tt),
        grid=(N // tm,),
        in_specs=[tok(WIDTH), tok(WIDTH), tok(D), mod, mod, mod, full(g2), full(wo), full(wr_hi), full(wr_lo)],
        out_specs=[tok(D), tok(D), tok(LANES)],
        out_shape=[jax.ShapeDtypeStruct((N, D), F32), jax.ShapeDtypeStruct((N, D), BF16),
                   jax.ShapeDtypeStruct((N, LANES), F32)],
        compiler_params=_params(("parallel",)),
        name="outproj_router",
    )(attn, ret, x2d, gate1, shift2, scale2, g2, wo, wr_hi, wr_lo)


def _moe_kernel(h_ref, comb_ref, x1_ref, gate_ref, wg_ref, wu_ref, wd_ref, y_ref, acc_scr, *, bb, tt):
    e = pl.program_id(1)
    D = x1_ref.shape[-1]

    @pl.when(e == 0)
    def _():
        acc_scr[...] = jnp.zeros(acc_scr.shape, F32)

    h = h_ref[...]
    a = jnp.dot(h, wg_ref[0], preferred_element_type=F32)
    b = jnp.dot(h, wu_ref[0], preferred_element_type=F32)
    mid = (_silu(a) * b).astype(BF16)
    out = jnp.dot(mid, wd_ref[0], preferred_element_type=F32)
    lane = lax.broadcasted_iota(I32, comb_ref.shape, 1)
    w = jnp.sum(jnp.where(lane == e + N_GROUPS, comb_ref[...], 0.0), axis=-1, keepdims=True)
    acc_scr[...] += w * out

    @pl.when(e == N_EXPERTS - 1)
    def _():
        y = x1_ref[...].reshape(bb, tt, D) + gate_ref[...] * acc_scr[...].reshape(bb, tt, D)
        y_ref[...] = y.reshape(bb * tt, D)


def _moe(h2, comb, x1, gate2, wg, wu, wd, *, bb, tt):
    N, D = x1.shape
    tm = bb * tt
    tok = lambda w: pl.BlockSpec((tm, w), lambda i, e: (i, 0))
    mod = pl.BlockSpec((bb, 1, D), lambda i, e: (i if bb > 1 else 0, 0, 0))
    return pl.pallas_call(
        functools.partial(_moe_kernel, bb=bb, tt=tt),
        grid=(N // tm, N_EXPERTS),
        in_specs=[tok(D), tok(LANES), tok(D), mod,
                  pl.BlockSpec((1, D, D_EXPERT), lambda i, e: (e, 0, 0)),
                  pl.BlockSpec((1, D, D_EXPERT), lambda i, e: (e, 0, 0)),
                  pl.BlockSpec((1, D_EXPERT, D), lambda i, e: (e, 0, 0))],
        out_specs=tok(D),
        out_shape=jax.ShapeDtypeStruct((N, D), F32),
        scratch_shapes=[pltpu.VMEM((tm, D), F32)],
        compiler_params=_params(("parallel", "arbitrary")),
        name="moe",
    )(h2, comb, x1, gate2, wg, wu, wd)


def _moe_routed_kernel(h_ref, comb_ref, x1_hbm, gate_ref, wg_ref, wu_ref, wd_ref, y_ref,
                       rank_scr, rankT_scr, gT_scr, sem):
    i = pl.program_id(0)
    e = pl.program_id(1)
    tm, D = y_ref.shape
    R, KP = MOE_ROWS, MOE_SCATTER_K

    @pl.when(e == 0)
    def _():
        residual = pltpu.make_async_copy(x1_hbm.at[pl.ds(i * tm, tm), :], y_ref, sem)
        residual.start()
        comb = comb_ref[...]
        used = comb != 0.0
        ti = lax.broadcasted_iota(I32, (tm, tm), 0)
        tj = lax.broadcasted_iota(I32, (tm, tm), 1)
        earlier = jnp.where(tj < ti, 1.0, 0.0).astype(BF16)
        rank = jnp.dot(earlier, jnp.where(used, 1.0, 0.0).astype(BF16), preferred_element_type=F32)
        rank = jnp.where(used, rank, -1.0)
        rank_scr[...] = rank
        rankT_scr[...] = rank.T
        gT_scr[...] = comb.T
        residual.wait()

    lane_e = e + N_GROUPS
    lane = lax.broadcasted_iota(I32, (tm, LANES), 1)
    r_col = jnp.sum(jnp.where(lane == lane_e, rank_scr[...], 0.0), axis=-1, keepdims=True)
    r_row = rankT_scr[pl.ds(lane_e, 1), :]
    g_row = gT_scr[pl.ds(lane_e, 1), :]
    n_e = jnp.sum(jnp.where(r_row >= 0.0, 1, 0))
    gate2 = gate_ref[0]
    CW = 512

    def chunk(c, carry):
        base = jnp.asarray(c * R, F32)
        ridx = base + lax.broadcasted_iota(I32, (R, tm), 0).astype(F32)
        pm = r_row == ridx
        x = jnp.dot(jnp.where(pm, 1.0, 0.0).astype(BF16), h_ref[...], preferred_element_type=F32).astype(BF16)
        a = jnp.dot(x, wg_ref[0], preferred_element_type=F32)
        b = jnp.dot(x, wu_ref[0], preferred_element_type=F32)
        mid = (_silu(a) * b).astype(BF16)
        y = jnp.dot(mid, wd_ref[0], preferred_element_type=F32)
        g_r = jnp.sum(jnp.where(pm, g_row, 0.0), axis=-1, keepdims=True)
        ys = (y * g_r * gate2).astype(BF16)
        ys = jnp.concatenate([ys, jnp.zeros((KP - R, D), BF16)], axis=0)
        cidx = lax.broadcasted_iota(I32, (tm, KP), 1)
        sm = (r_col == base + cidx.astype(F32)) & (cidx < R)
        s = jnp.where(sm, 1.0, 0.0).astype(BF16)
        for c0 in range(0, D, CW):
            y_ref[:, c0:c0 + CW] += jnp.dot(s, ys[:, c0:c0 + CW], preferred_element_type=F32)
        return carry

    lax.fori_loop(0, (n_e + R - 1) // R, chunk, 0)


def _moe_routed(h2, comb, x1, gate2, wg, wu, wd):
    N, D = x1.shape
    tm = MOE_TOKENS
    assert gate2.shape[0] == 1 and N % tm == 0
    tok = lambda w: pl.BlockSpec((tm, w), lambda i, e: (i, 0))
    return pl.pallas_call(
        _moe_routed_kernel,
        grid=(N // tm, N_EXPERTS),
        in_specs=[tok(D), tok(LANES), pl.BlockSpec(memory_space=pl.ANY),
                  pl.BlockSpec((1, 1, D), lambda i, e: (0, 0, 0)),
                  pl.BlockSpec((1, D, D_EXPERT), lambda i, e: (e, 0, 0)),
                  pl.BlockSpec((1, D, D_EXPERT), lambda i, e: (e, 0, 0)),
                  pl.BlockSpec((1, D_EXPERT, D), lambda i, e: (e, 0, 0))],
        out_specs=tok(D),
        out_shape=jax.ShapeDtypeStruct((N, D), F32),
        scratch_shapes=[pltpu.VMEM((tm, LANES), F32), pltpu.VMEM((LANES, tm), F32),
                        pltpu.VMEM((LANES, tm), F32), pltpu.SemaphoreType.DMA(())],
        compiler_params=_params(("parallel", "arbitrary")),
        name="moe_routed",
    )(h2, comb, x1, gate2, wg, wu, wd)


def _layer(x, mod, past, W, *, bb, tt, TQ, TK, KC, ret_chunk):
    B, T, D = x.shape
    N = B * T
    x2d = x.reshape(N, D)
    shift1, scale1, gate1, shift2, scale2, gate2 = [m.reshape(B, 1, D) for m in jnp.split(mod, 6, axis=-1)]
    z = _inproj(x2d, shift1, scale1, W["g1"], W["w_in"], W["qg"], W["kg"], bb=bb, tt=tt)
    vT_new = z["avT"].reshape(N_HEADS, VT_ROWS, B, T)
    vT_new = vT_new.reshape(1, N_HEADS, VT_ROWS, T) if B == 1 else vT_new.transpose(2, 0, 1, 3)

    if past is None:
        state0 = jnp.zeros((B, N_HEADS, HEAD_DIM, HEAD_DIM), F32)
        P, Lreal = 0, T
        topk = min(TOPK_MAX, T // 4)
        kb = z["akb"].reshape(B, T, WIDTH)
        vT = vT_new
        ikb = z["ikb"].reshape(B, T, IDX_DIM)
        Tq = T
        pad_q = lambda a: a.reshape(B, T, a.shape[-1])
    else:
        ck, cv, cki, state0 = past
        P = ck.shape[1]
        Lreal = P + T
        topk = min(TOPK_MAX, Lreal // 4)
        Lp = -(-Lreal // TK) * TK
        kb, vT, ikb = _assemble_cache(ck, cv, cki, z["akb"].reshape(B, T, WIDTH), vT_new,
                                      z["ikb"].reshape(B, T, IDX_DIM), Lp)
        Tq = -(-T // TQ) * TQ
        pad_q = lambda a: jnp.pad(a.reshape(B, T, a.shape[-1]), ((0, 0), (0, Tq - T), (0, 0)))

    iq = pad_q(z["iq"])
    iwT = pad_q(z["iw"]).reshape(B, Tq // TQ, TQ, IDX_HEADS).transpose(0, 1, 3, 2)
    keysT, tn = _index_select(iq, ikb, iwT, TQ=TQ, TK=TK, P=P, Lreal=Lreal, topk=topk)
    attn = _attention(pad_q(z["aq"]), kb, vT, keysT, tn, TQ=TQ, TK=TK, KC=KC, P=P, Lreal=Lreal)
    attn = attn[:, :T].reshape(N, WIDTH)

    ret, ret_state = _retention(z["rq"], z["rk"], z["rv"], z["rg"], state0, B=B, T=T, C=ret_chunk)
    x1, h2, comb = _outproj(attn, ret, x2d, gate1, shift2, scale2, W["g2"], W["w_out"], W["w_r_hi"],
                            W["w_r_lo"], bb=bb, tt=tt)
    if B == 1 and N % MOE_TOKENS == 0:
        y = _moe_routed(h2, comb, x1, gate2, W["wg"], W["wu"], W["wd"])
    else:
        y = _moe(h2, comb, x1, gate2, W["wg"], W["wu"], W["wd"], bb=bb, tt=tt)
    return (y.reshape(B, T, D), z["ak"].reshape(B, T, N_HEADS, HEAD_DIM),
            z["av"].reshape(B, T, N_HEADS, HEAD_DIM), z["ik"].reshape(B, T, IDX_DIM), ret_state)


def kernel(x_prompt, x_sample, cache_k, cache_v, cache_kidx, state_ret, c_prompt, c_sample, w_ada, b_ada,
           norm1_g, norm2_g, w_in, q_norm_g, k_norm_g, w_out, w_group, w_router, w_gate_e, w_up_e, w_down_e):
    depth = w_ada.shape[0]
    D = D_MODEL
    Bp, Bs = x_prompt.shape[0], x_sample.shape[0]
    Ts = x_sample.shape[1]
    xp, xs = x_prompt, x_sample
    outs_p, outs_s = [], []
    for l in range(depth):
        w_r = jnp.concatenate(
            [w_group[l], jnp.moveaxis(w_router[l], 0, 1).reshape(D, N_EXPERTS),
             jnp.zeros((D, LANES - N_GROUPS - N_EXPERTS), F32)], axis=1)
        w_r_hi = w_r.astype(BF16)
        W = {
            "g1": norm1_g[l].reshape(1, D), "g2": norm2_g[l].reshape(1, D),
            "qg": q_norm_g[l].reshape(1, HEAD_DIM), "kg": k_norm_g[l].reshape(1, HEAD_DIM),
            "w_in": _pad_cast_w_in(w_in[l]),
            "w_out": w_out[l].astype(BF16),
            "w_r_hi": w_r_hi, "w_r_lo": (w_r - w_r_hi.astype(F32)).astype(BF16),
            "wg": w_gate_e[l].astype(BF16), "wu": w_up_e[l].astype(BF16), "wd": w_down_e[l].astype(BF16),
        }
        rows = Bp + Bs
        rows_p = -(-rows // 8) * 8
        c_all = jnp.concatenate([c_prompt, c_sample, jnp.zeros((rows_p - rows, D), F32)], axis=0)
        mod = _adaln_mod(c_all, w_ada[l], b_ada[l])
        past = (cache_k[l], cache_v[l], cache_kidx[l], state_ret[l])
        xp, kp, vp, kip, sp = _layer(xp, mod[:Bp], None, W, bb=1, tt=512, TQ=256, TK=512, KC=256,
                                     ret_chunk=256)
        xs, kn, vn, kin, sn = _layer(xs, mod[Bp:rows], past, W, bb=512 // Ts, tt=Ts, TQ=128, TK=384, KC=384,
                                     ret_chunk=Ts)
        outs_p.append((kp, vp, kip, sp))
        outs_s.append((kn, vn, kin, sn))
    st = lambda xs_, i: jnp.stack([o[i] for o in xs_])
    return (xp, xs, st(outs_p, 0), st(outs_p, 1), st(outs_p, 2), st(outs_p, 3),
            st(outs_s, 0), st(outs_s, 1), st(outs_s, 2), st(outs_s, 3))
```

```python
import functools
import math

import numpy as np
import jax
import jax.numpy as jnp
from jax import lax
from jax.experimental import pallas as pl
from jax.experimental.pallas import tpu as pltpu

F32 = jnp.float32
BF16 = jnp.bfloat16
I32 = jnp.int32

D_MODEL = 2048
CHUNK = 64
N_HEADS = 8
HEAD_DIM = 128
WIDTH = N_HEADS * HEAD_DIM
IDX_HEADS = 8
IDX_DIM = 64
TOPK_MAX = 256
ATTN_SCALE = HEAD_DIM ** -0.5
IDX_SCALE = IDX_DIM ** -0.5
IDX_W_SCALE = IDX_HEADS ** -0.5
RET_K_SCALE = HEAD_DIM ** -0.5
N_GROUPS = 4
EXPERTS_PER_GROUP = 4
N_EXPERTS = N_GROUPS * EXPERTS_PER_GROUP
D_EXPERT = 512
EPS = 1e-6

OFF_IQ = 3 * WIDTH
IDX_COLS = IDX_HEADS * IDX_DIM + IDX_DIM + IDX_HEADS
OFF_RQ = OFF_IQ + IDX_COLS
N_COL_GROUPS = 8

LANES = 128
INT_MIN = -(2 ** 31)
TAKE_ALL = 2 ** 30
MASKED_DIST = 1e33
VMEM_LIMIT = 56 * 1024 * 1024
MOE_TOKENS = 1024
MOE_ROWS = 192
MOE_SCATTER_K = 256

LOG2E = 1.4426950408889634
VT_ROWS = HEAD_DIM + 16
ALIBI_SLOPES = [float(2.0 ** (-8.0 * (h + 1) / N_HEADS)) for h in range(N_HEADS)]
LOG_GAMMA = [float(np.log1p(-(2.0 ** (-5.0 - h)))) for h in range(N_HEADS)]


def _params(sem):
    return pltpu.CompilerParams(dimension_semantics=sem, vmem_limit_bytes=VMEM_LIMIT)


def _silu(x):
    return x * jax.nn.sigmoid(x)


def _mod_kernel(c_ref, w_ref, b_ref, o_ref):
    s = _silu(c_ref[...])
    o_ref[...] = jnp.dot(s, w_ref[...], precision=lax.Precision.HIGHEST,
                         preferred_element_type=F32) + b_ref[...]


def _adaln_mod(c, w_ada, b_ada):
    R, D = c.shape
    N = w_ada.shape[1]
    tn = 1024
    return pl.pallas_call(
        _mod_kernel,
        grid=(N // tn,),
        in_specs=[pl.BlockSpec((R, D), lambda j: (0, 0)),
                  pl.BlockSpec((D, tn), lambda j: (0, j)),
                  pl.BlockSpec((1, tn), lambda j: (0, j))],
        out_specs=pl.BlockSpec((R, tn), lambda j: (0, j)),
        out_shape=jax.ShapeDtypeStruct((R, N), F32),
        compiler_params=_params(("arbitrary",)),
        name="adaln_mod",
    )(c, w_ada, b_ada.reshape(1, N))


def _w_in_kernel(w_ref, o_ref):
    rows = w_ref.shape[0]
    pad = o_ref.shape[1] - w_ref.shape[1]
    o_ref[:, 0:OFF_RQ] = w_ref[:, 0:OFF_RQ].astype(BF16)
    o_ref[:, OFF_RQ:OFF_RQ + pad] = jnp.zeros((rows, pad), BF16)
    o_ref[:, OFF_RQ + pad:] = w_ref[:, OFF_RQ:].astype(BF16)


def _pad_cast_w_in(w):
    D, n_in = w.shape
    n_out = N_COL_GROUPS * WIDTH
    rows = 128
    return pl.pallas_call(
        _w_in_kernel,
        grid=(D // rows,),
        in_specs=[pl.BlockSpec((rows, n_in), lambda i: (i, 0))],
        out_specs=pl.BlockSpec((rows, n_out), lambda i: (i, 0)),
        out_shape=jax.ShapeDtypeStruct((D, n_out), BF16),
        compiler_params=_params(("parallel",)),
        name="pad_cast_w_in",
    )(w)


def _inproj_kernel(x_ref, shift_ref, scale_ref, g1_ref, w_ref, qg_ref, kg_ref,
                   aq_ref, ak_ref, akb_ref, av_ref, avT_ref, iq_ref, ik_ref, ikb_ref, iw_ref,
                   rq_ref, rk_ref, rv_ref, rg_ref, h_scr, *, bb, tt):
    j = pl.program_id(1)
    D = x_ref.shape[-1]

    @pl.when(j == 0)
    def _():
        x = x_ref[...]
        ms = jnp.mean(x * x, axis=-1, keepdims=True)
        y = x * lax.rsqrt(ms + EPS) * g1_ref[...]
        y = y.reshape(bb, tt, D) * (1.0 + scale_ref[...]) + shift_ref[...]
        h_scr[...] = y.reshape(bb * tt, D).astype(BF16)

    CW = 2 * HEAD_DIM
    chunks = [slice(c * CW, (c + 1) * CW) for c in range(WIDTH // CW)]

    def proj(cols):
        return jnp.dot(h_scr[...], w_ref[:, cols], preferred_element_type=F32)

    def head_rms(zc, g_ref):
        outs = []
        for hh in range(CW // HEAD_DIM):
            zh = zc[:, hh * HEAD_DIM:(hh + 1) * HEAD_DIM]
            ms = jnp.mean(zh * zh, axis=-1, keepdims=True)
            outs.append(zh * lax.rsqrt(ms + EPS) * g_ref[...])
        return jnp.concatenate(outs, axis=1)

    @pl.when(j == 0)
    def _():
        for cols in chunks:
            v = head_rms(proj(cols), qg_ref)
            aq_ref[:, cols] = (v * (ATTN_SCALE * LOG2E)).astype(BF16)

    @pl.when(j == 1)
    def _():
        for cols in chunks:
            v = head_rms(proj(cols), kg_ref)
            ak_ref[:, cols] = v
            akb_ref[:, cols] = v.astype(BF16)

    @pl.when(j == 2)
    def _():
        for c, cols in enumerate(chunks):
            zc = proj(cols)
            av_ref[:, cols] = zc
            zT = zc.T
            for hh in range(CW // HEAD_DIM):
                h = c * (CW // HEAD_DIM) + hh
                avT_ref[h, 0:HEAD_DIM, :] = zT[hh * HEAD_DIM:(hh + 1) * HEAD_DIM, :].astype(BF16)
                avT_ref[h, HEAD_DIM:VT_ROWS, :] = jnp.ones((VT_ROWS - HEAD_DIM, zT.shape[1]), BF16)

    @pl.when(j == 3)
    def _():
        nq = IDX_HEADS * IDX_DIM
        for cols in chunks[:nq // CW]:
            iq_ref[:, cols] = (proj(cols) * IDX_SCALE).astype(BF16)
        zc = proj(chunks[nq // CW])
        ik = zc[:, 0:IDX_DIM]
        ik_ref[...] = ik
        ikb_ref[...] = ik.astype(BF16)
        iw_ref[...] = zc[:, IDX_DIM:IDX_DIM + IDX_HEADS] * IDX_W_SCALE

    @pl.when(j == 4)
    def _():
        for cols in chunks:
            rq_ref[:, cols] = proj(cols).astype(BF16)

    @pl.when(j == 5)
    def _():
        for cols in chunks:
            rk_ref[:, cols] = (proj(cols) * RET_K_SCALE).astype(BF16)

    @pl.when(j == 6)
    def _():
        for cols in chunks:
            rv_ref[:, cols] = proj(cols).astype(BF16)

    @pl.when(j == 7)
    def _():
        for cols in chunks:
            rg_ref[:, cols] = proj(cols)


def _inproj(x2d, shift, scale, g1, w_p, qg, kg, *, bb, tt):
    N, D = x2d.shape
    tm = bb * tt
    nI = N // tm
    tok = lambda w: pl.BlockSpec((tm, w), lambda i, j: (i, 0))
    mod = pl.BlockSpec((bb, 1, D), lambda i, j: (i if bb > 1 else 0, 0, 0))
    row = lambda w: pl.BlockSpec((1, w), lambda i, j: (0, 0))
    outs = [("aq", WIDTH, BF16), ("ak", WIDTH, F32), ("akb", WIDTH, BF16), ("av", WIDTH, F32),
            ("avT", None, BF16), ("iq", IDX_HEADS * IDX_DIM, BF16), ("ik", IDX_DIM, F32),
            ("ikb", IDX_DIM, BF16), ("iw", IDX_HEADS, F32), ("rq", WIDTH, BF16), ("rk", WIDTH, BF16),
            ("rv", WIDTH, BF16), ("rg", WIDTH, F32)]
    out_specs = [tok(w) if w else pl.BlockSpec((N_HEADS, VT_ROWS, tm), lambda i, j: (0, 0, i))
                 for _, w, _ in outs]
    out_shape = [jax.ShapeDtypeStruct((N, w) if w else (N_HEADS, VT_ROWS, N), dt) for _, w, dt in outs]
    res = pl.pallas_call(
        functools.partial(_inproj_kernel, bb=bb, tt=tt),
        grid=(nI, N_COL_GROUPS),
        in_specs=[tok(D), mod, mod, row(D),
                  pl.BlockSpec((D, WIDTH), lambda i, j: (0, j)), row(HEAD_DIM), row(HEAD_DIM)],
        out_specs=out_specs,
        out_shape=out_shape,
        scratch_shapes=[pltpu.VMEM((tm, D), BF16)],
        compiler_params=_params(("parallel", "arbitrary")),
        name="inproj",
    )(x2d, shift, scale, g1, w_p, qg, kg)
    return {name: r for (name, _, _), r in zip(outs, res)}


def _ret_kernel(q_ref, k_ref, v_ref, g_ref, s0_ref, o_ref, sn_ref, st_scr, *, C):
    c = pl.program_id(1)

    @pl.when(c == 0)
    def _():
        st_scr[...] = s0_ref[0]

    pi = lax.broadcasted_iota(I32, (C, C), 0)
    pj = lax.broadcasted_iota(I32, (C, C), 1)
    diff = (pi - pj).astype(F32)
    causal = pi >= pj
    pos = lax.broadcasted_iota(I32, (C, HEAD_DIM), 0).astype(F32)
    for h in range(N_HEADS):
        lg = LOG_GAMMA[h]
        sl = slice(h * HEAD_DIM, (h + 1) * HEAD_DIM)
        q = q_ref[:, sl]
        k = k_ref[:, sl]
        v = v_ref[:, sl]
        decay = jnp.where(causal, jnp.exp(lg * jnp.maximum(diff, 0.0)), 0.0)
        s = lax.dot_general(q, k, (((1,), (1,)), ((), ())), preferred_element_type=F32) * decay
        o = jnp.dot(s.astype(BF16), v, preferred_element_type=F32)
        st = st_scr[h]
        cross = jnp.exp(lg * (pos + 1.0))
        o = o + jnp.dot(q, st.astype(BF16), preferred_element_type=F32) * cross
        kdec = jnp.exp(lg * (C - 1.0 - pos))
        kd = (k.astype(F32) * kdec).astype(BF16)
        st_new = math.exp(lg * C) * st + lax.dot_general(
            kd, v, (((0,), (0,)), ((), ())), preferred_element_type=F32)
        st_scr[h] = st_new
        ms = jnp.mean(o * o, axis=-1, keepdims=True)
        o_ref[:, sl] = (o * lax.rsqrt(ms + EPS) * _silu(g_ref[:, sl])).astype(BF16)

    @pl.when(c == pl.num_programs(1) - 1)
    def _():
        sn_ref[0] = st_scr[...]


def _retention(rq, rk, rv, rg, state0, *, B, T, C):
    nC = T // C
    tok = pl.BlockSpec((C, WIDTH), lambda b, c: (b * nC + c, 0))
    st = pl.BlockSpec((1, N_HEADS, HEAD_DIM, HEAD_DIM), lambda b, c: (b, 0, 0, 0))
    return pl.pallas_call(
        functools.partial(_ret_kernel, C=C),
        grid=(B, nC),
        in_specs=[tok, tok, tok, tok, st],
        out_specs=[tok, st],
        out_shape=[jax.ShapeDtypeStruct((B * T, WIDTH), BF16),
                   jax.ShapeDtypeStruct((B, N_HEADS, HEAD_DIM, HEAD_DIM), F32)],
        scratch_shapes=[pltpu.VMEM((N_HEADS, HEAD_DIM, HEAD_DIM), F32)],
        compiler_params=_params(("parallel", "arbitrary")),
        name="retention",
    )(rq, rk, rv, rg, state0)


def _cache_kernel(ck_ref, cv_ref, cki_ref, kn_ref, vn_ref, in_ref, kb_ref, vT_ref, ikb_ref):
    P = cki_ref.shape[1]
    T = kn_ref.shape[1]
    Lp = kb_ref.shape[1]
    for h in range(N_HEADS):
        sl = slice(h * HEAD_DIM, (h + 1) * HEAD_DIM)
        kb_ref[0, 0:P, sl] = ck_ref[0, pl.ds(h, P, stride=N_HEADS), :].astype(BF16)
        vT_ref[0, h, 0:HEAD_DIM, 0:P] = cv_ref[0, pl.ds(h, P, stride=N_HEADS), :].T.astype(BF16)
        vT_ref[0, h, HEAD_DIM:VT_ROWS, 0:P] = jnp.ones((VT_ROWS - HEAD_DIM, P), BF16)
        vT_ref[0, h, :, P:Lp] = jnp.concatenate(
            [vn_ref[0, h], jnp.zeros((VT_ROWS, Lp - P - T), BF16)], axis=1)
    kb_ref[0, P:P + T, :] = kn_ref[0]
    kb_ref[0, P + T:Lp, :] = jnp.zeros((Lp - P - T, WIDTH), BF16)
    ikb_ref[0, 0:P, :] = cki_ref[0].astype(BF16)
    ikb_ref[0, P:P + T, :] = in_ref[0]
    ikb_ref[0, P + T:Lp, :] = jnp.zeros((Lp - P - T, IDX_DIM), BF16)


def _assemble_cache(ck, cv, cki, k_new, vT_new, ik_new, Lp):
    B, P = ck.shape[:2]
    T = k_new.shape[1]
    lead = lambda *blk: pl.BlockSpec((1,) + blk, lambda b: (b,) + (0,) * len(blk))
    return pl.pallas_call(
        _cache_kernel,
        grid=(B,),
        in_specs=[lead(P * N_HEADS, HEAD_DIM), lead(P * N_HEADS, HEAD_DIM), lead(P, IDX_DIM),
                  lead(T, WIDTH), lead(N_HEADS, VT_ROWS, T), lead(T, IDX_DIM)],
        out_specs=[lead(Lp, WIDTH), lead(N_HEADS, VT_ROWS, Lp), lead(Lp, IDX_DIM)],
        out_shape=[jax.ShapeDtypeStruct((B, Lp, WIDTH), BF16),
                   jax.ShapeDtypeStruct((B, N_HEADS, VT_ROWS, Lp), BF16),
                   jax.ShapeDtypeStruct((B, Lp, IDX_DIM), BF16)],
        compiler_params=_params(("parallel",)),
        name="assemble_cache",
    )(ck.reshape(B, P * N_HEADS, HEAD_DIM), cv.reshape(B, P * N_HEADS, HEAD_DIM), cki, k_new, vT_new, ik_new)


def _key_limit(q_first, q_count, P, Lreal):
    return jnp.minimum(Lreal, ((P + q_first + q_count - 1) // CHUNK + 1) * CHUNK)


def _sorting_program(n, keep):
    pairs = []

    def merge(lo, m, r):
        step = r * 2
        if step < m:
            merge(lo, m, step)
            merge(lo + r, m, step)
            pairs.extend((i, i + r) for i in range(lo + r, lo + m - r, step))
        else:
            pairs.append((lo, lo + r))

    def sort(lo, m):
        if m > 1:
            sort(lo, m // 2)
            sort(lo + m // 2, m // 2)
            merge(lo, m, 1)

    sort(0, n)
    need, prog = set(range(keep)), []
    for i, j in reversed(pairs):
        if i in need or j in need:
            prog.append((i, j, i in need, j in need))
            need.update((i, j))
    return prog[::-1]


SORT_GROUP = 16
SORT_LEVELS = 8
SORT_MIN_KEYS = 6400
SORT_PROGRAM = _sorting_program(SORT_GROUP, SORT_LEVELS)


def _idx_kernel(iq_ref, ik_ref, iwT_ref, keys_ref, tn_ref, lev_scr, *, TQ, TK, P, Lreal, topk):
    qb = pl.program_id(1)
    nkt = (_key_limit(qb * TQ, TQ, P, Lreal) + TK - 1) // TK
    tq = P + qb * TQ + lax.broadcasted_iota(I32, (TK, TQ), 1)
    lrow = lax.broadcasted_iota(I32, (TK, TQ), 0)
    LR = TK // SORT_GROUP
    use_levels = nkt * TK >= SORT_MIN_KEYS

    def tile_body(kt, carry):
        r0 = pl.multiple_of(kt * TK, TK)
        ik = ik_ref[0, pl.ds(r0, TK), :]
        acc = jnp.zeros((TK, TQ), F32)
        for h in range(IDX_HEADS):
            qh = iq_ref[0, :, h * IDX_DIM:(h + 1) * IDX_DIM]
            r = lax.dot_general(ik, qh, (((1,), (1,)), ((), ())), preferred_element_type=F32)
            acc = acc + jnp.maximum(r, 0.0) * iwT_ref[0, 0, h:h + 1, :]
        l = r0 + lrow
        allowed = (l < Lreal) & ((l >> 6) <= (tq >> 6))
        bits = pltpu.bitcast(acc, I32)
        key = bits ^ ((bits >> 31) & 0x7FFFFFFF)
        key = jnp.where(allowed, key, INT_MIN)
        keys_ref[0, 0, pl.ds(r0, TK), :] = key

        @pl.when(use_levels)
        def _():
            x = key.reshape(TK // (SORT_GROUP * 8), SORT_GROUP, 8, TQ)
            w = [x[:, j] for j in range(SORT_GROUP)]
            for i, j, need_max, need_min in SORT_PROGRAM:
                hi, lo = jnp.maximum(w[i], w[j]), jnp.minimum(w[i], w[j])
                if need_max:
                    w[i] = hi
                if need_min:
                    w[j] = lo
            for v in range(SORT_LEVELS):
                lev_scr[v, pl.ds(pl.multiple_of(kt * LR, LR), LR), :] = w[v].reshape(LR, TQ)
        return carry

    lax.fori_loop(0, nkt, tile_body, 0)

    def fill_body(kt, carry):
        keys_ref[0, 0, pl.ds(pl.multiple_of(kt * TK, TK), TK), :] = jnp.full((TK, TQ), INT_MIN, I32)
        return carry

    lax.fori_loop(nkt, keys_ref.shape[2] // TK, fill_body, 0)

    SUB = 64

    def count(pred):
        def body(kt, acc):
            r0 = pl.multiple_of(kt * TK, TK)
            for s in range(TK // SUB):
                blk = keys_ref[0, 0, pl.ds(r0 + s * SUB, SUB), :]
                acc = acc + pred(blk).astype(I32)
            return acc
        acc = lax.fori_loop(0, nkt, body, jnp.zeros((SUB, TQ), I32))
        return jnp.sum(acc, axis=0, keepdims=True)

    def count_levels(pred, levels):
        def body(kt, acc):
            r0 = pl.multiple_of(kt * LR, LR)
            for v in levels:
                acc = acc + pred(lev_scr[v, pl.ds(r0, LR), :]).astype(I32)
            return acc
        acc = lax.fori_loop(0, nkt, body, jnp.zeros((LR, TQ), I32))
        return jnp.sum(acc, axis=0, keepdims=True)

    def search(count_fn):
        def pass_body(b, carry):
            t_u, cnt_t = carry
            cand_u = t_u | lax.shift_left(jnp.int32(1), jnp.asarray(31 - b, I32))
            cand_s = cand_u ^ INT_MIN
            cnt = count_fn(lambda blk: blk >= cand_s)
            take = cnt >= topk
            return jnp.where(take, cand_u, t_u), jnp.where(take, cnt, cnt_t)
        return lax.fori_loop(0, 32, pass_body,
                             (jnp.zeros((1, TQ), I32), jnp.full((1, TQ), TAKE_ALL, I32)))

    scanned = range(SORT_LEVELS - 1)

    def search_levels():
        t_u, cnt_t = search(lambda pred: count_levels(pred, scanned))
        t_s = t_u ^ INT_MIN
        missed = count_levels(lambda blk: blk >= t_s, [SORT_LEVELS - 1])
        return t_u, cnt_t, jnp.max(missed)

    t_u, cnt_t, missed = lax.cond(
        use_levels, search_levels,
        lambda: (jnp.zeros((1, TQ), I32), jnp.full((1, TQ), TAKE_ALL, I32), jnp.int32(1)))
    full_scan = missed > 0
    t_u, cnt_t = lax.cond(full_scan, lambda: search(count), lambda: (t_u, cnt_t))
    t_s = t_u ^ INT_MIN
    ties = (cnt_t > topk) & (t_s != INT_MIN)
    any_ties = jnp.max(ties.astype(I32)) > 0
    count_gt = lambda: lax.cond(full_scan, lambda: count(lambda blk: blk > t_s),
                                lambda: count_levels(lambda blk: blk > t_s, scanned))
    cnt_gt = lax.cond(any_ties, count_gt, lambda: jnp.zeros((1, TQ), I32))
    n_take = jnp.where(ties, topk - cnt_gt, TAKE_ALL)
    row = lax.broadcasted_iota(I32, (8, TQ), 0)
    tn_ref[0, 0] = jnp.where(row == 0, t_s, n_take)


def _index_select(iq, ikb, iwT, *, TQ, TK, P, Lreal, topk):
    nB, Lp, _ = ikb.shape
    nQ = iwT.shape[1]
    return pl.pallas_call(
        functools.partial(_idx_kernel, TQ=TQ, TK=TK, P=P, Lreal=Lreal, topk=topk),
        grid=(nB, nQ),
        in_specs=[pl.BlockSpec((1, TQ, IDX_HEADS * IDX_DIM), lambda b, q: (b, q, 0)),
                  pl.BlockSpec((1, Lp, IDX_DIM), lambda b, q: (b, 0, 0)),
                  pl.BlockSpec((1, 1, IDX_HEADS, TQ), lambda b, q: (b, q, 0, 0))],
        out_specs=[pl.BlockSpec((1, 1, Lp, TQ), lambda b, q: (b, q, 0, 0)),
                   pl.BlockSpec((1, 1, 8, TQ), lambda b, q: (b, q, 0, 0))],
        out_shape=[jax.ShapeDtypeStruct((nB, nQ, Lp, TQ), I32),
                   jax.ShapeDtypeStruct((nB, nQ, 8, TQ), I32)],
        scratch_shapes=[pltpu.VMEM((SORT_LEVELS, Lp // SORT_GROUP, TQ), I32)],
        compiler_params=_params(("parallel", "arbitrary")),
        name="index_select",
    )(iq, ikb, iwT)


def _attn_kernel(qbs_ref, kts_ref, last_ref, q_ref, k_ref, vT_ref, keys_ref, tn_ref, o_ref,
                 qT_scr, ndm_scr, a_scr, m_scr, l_scr, acc_scr, tie_scr, *, TQ, TK, KC, P, Lreal):
    s_id = pl.program_id(1)
    qb = qbs_ref[s_id]
    kt = kts_ref[s_id]
    is_last = last_ref[s_id] == 1
    heads = [slice(h * HEAD_DIM, (h + 1) * HEAD_DIM) for h in range(N_HEADS)]

    @pl.when(kt == 0)
    def _():
        m_scr[...] = jnp.full(m_scr.shape, -jnp.inf, F32)
        l_scr[...] = jnp.zeros(l_scr.shape, F32)
        acc_scr[...] = jnp.zeros(acc_scr.shape, F32)
        tie_scr[...] = jnp.zeros(tie_scr.shape, F32)
        for sl in heads:
            qT_scr[sl, :] = q_ref[0, :, sl].astype(F32).T.astype(BF16)

    keys = keys_ref[0, 0]
    thr = tn_ref[0, 0, 0:1, :]
    ntk = tn_ref[0, 0, 1:2, :]
    l = kt * TK + lax.broadcasted_iota(I32, (TK, TQ), 0)
    tq = P + qb * TQ + lax.broadcasted_iota(I32, (TK, TQ), 1)
    has_ties = jnp.max(jnp.where(ntk == TAKE_ALL, 0, 1)) > 0
    no_ties = jnp.logical_not(has_ties)

    @pl.when(jnp.logical_and(no_ties, jnp.logical_not(is_last)))
    def _():
        ndm_scr[...] = jnp.where(keys >= thr, (l - tq).astype(F32), -MASKED_DIST)

    @pl.when(jnp.logical_and(no_ties, is_last))
    def _():
        allowed = (l < Lreal) & ((l >> 6) <= (tq >> 6))
        ndm_scr[...] = jnp.where((keys >= thr) & allowed, -jnp.abs(tq - l).astype(F32), -MASKED_DIST)

    @pl.when(has_ties)
    def _():
        allowed = (l < Lreal) & ((l >> 6) <= (tq >> 6))
        eq = keys == thr
        li = lax.broadcasted_iota(I32, (TK, TK), 0)
        lj = lax.broadcasted_iota(I32, (TK, TK), 1)
        lower = jnp.where(lj <= li, 1.0, 0.0).astype(BF16)
        prefix = jnp.dot(lower, jnp.where(eq, 1.0, 0.0).astype(BF16), preferred_element_type=F32)
        rank = tie_scr[0:1, :] + prefix
        sel = ((keys > thr) | (eq & (rank <= ntk.astype(F32)))) & allowed
        ndm_scr[...] = jnp.where(sel, -jnp.abs(tq - l).astype(F32), -MASKED_DIST)
        tie_scr[0:1, :] = tie_scr[0:1, :] + prefix[TK - 1:TK, :]

    chunks = [slice(c * KC, (c + 1) * KC) for c in range(TK // KC)]

    def stage_a(h, rows, mx):
        s = jnp.dot(k_ref[0, rows, heads[h]], qT_scr[heads[h], :], preferred_element_type=F32)
        a = s + (ALIBI_SLOPES[h] * LOG2E) * ndm_scr[rows, :]
        a_scr[h % 2, rows, :] = a
        cm = jnp.max(a, axis=0, keepdims=True)
        return cm if mx is None else jnp.maximum(mx, cm)

    def stage_b(h, rows, m_new, pv):
        p = jnp.exp2(a_scr[h % 2, rows, :] - m_new).astype(BF16)
        d = jnp.dot(vT_ref[0, h, :, rows], p, preferred_element_type=F32)
        return d if pv is None else pv + d

    m_all = m_scr[...]
    l_all = l_scr[...]
    m_out, l_out = [], []
    mx = None
    for rows in chunks:
        mx = stage_a(0, rows, mx)
    for h in range(N_HEADS):
        m_prev = m_all[h:h + 1]
        m_new = jnp.maximum(m_prev, mx)
        alpha = jnp.exp2(m_prev - m_new)
        mx, pv = None, None
        for rows in chunks:
            if h + 1 < N_HEADS:
                mx = stage_a(h + 1, rows, mx)
            pv = stage_b(h, rows, m_new, pv)
        acc_scr[heads[h], :] = alpha * acc_scr[heads[h], :] + pv[0:HEAD_DIM]
        l_out.append(alpha * l_all[h:h + 1] + pv[HEAD_DIM:HEAD_DIM + 1])
        m_out.append(m_new)
    m_scr[...] = jnp.concatenate(m_out, axis=0)
    l_scr[...] = jnp.concatenate(l_out, axis=0)

    @pl.when(is_last)
    def _():
        for h in range(N_HEADS):
            o_ref[0, :, heads[h]] = (acc_scr[heads[h], :] / l_scr[h:h + 1, :]).T.astype(BF16)


def _attention(aq, kb, vT, keysT, tn, *, TQ, TK, KC, P, Lreal):
    nB, Tq, _ = aq.shape
    nQ = Tq // TQ
    qbs, kts, last = [], [], []
    for qb in range(nQ):
        limit = min(Lreal, ((P + qb * TQ + TQ - 1) // CHUNK + 1) * CHUNK)
        n = -(-limit // TK)
        assert (n - 1) * TK <= P + qb * TQ
        qbs += [qb] * n
        kts += list(range(n))
        last += [0] * (n - 1) + [1]
    steps = len(qbs)
    qmap = lambda b, s, qbs, kts, last: (b, qbs[s], 0)
    grid_spec = pltpu.PrefetchScalarGridSpec(
        num_scalar_prefetch=3,
        grid=(nB, steps),
        in_specs=[pl.BlockSpec((1, TQ, WIDTH), qmap),
                  pl.BlockSpec((1, TK, WIDTH), lambda b, s, qbs, kts, last: (b, kts[s], 0)),
                  pl.BlockSpec((1, N_HEADS, VT_ROWS, TK), lambda b, s, qbs, kts, last: (b, 0, 0, kts[s])),
                  pl.BlockSpec((1, 1, TK, TQ), lambda b, s, qbs, kts, last: (b, qbs[s], kts[s], 0)),
                  pl.BlockSpec((1, 1, 8, TQ), lambda b, s, qbs, kts, last: (b, qbs[s], 0, 0))],
        out_specs=pl.BlockSpec((1, TQ, WIDTH), qmap),
        scratch_shapes=[pltpu.VMEM((WIDTH, TQ), BF16),
                        pltpu.VMEM((TK, TQ), F32),
                        pltpu.VMEM((2, TK, TQ), F32),
                        pltpu.VMEM((N_HEADS, TQ), F32),
                        pltpu.VMEM((N_HEADS, TQ), F32),
                        pltpu.VMEM((WIDTH, TQ), F32),
                        pltpu.VMEM((8, TQ), F32)])
    arr = lambda v: jnp.asarray(np.array(v, np.int32))
    return pl.pallas_call(
        functools.partial(_attn_kernel, TQ=TQ, TK=TK, KC=KC, P=P, Lreal=Lreal),
        grid_spec=grid_spec,
        out_shape=jax.ShapeDtypeStruct((nB, Tq, WIDTH), BF16),
        compiler_params=_params(("parallel", "arbitrary")),
        name="sparse_attention",
    )(arr(qbs), arr(kts), arr(last), aq, kb, vT, keysT, tn)


def _outproj_kernel(attn_ref, ret_ref, x_ref, gate_ref, shift_ref, scale_ref, g2_ref, wo_ref, wrh_ref, wrl_ref,
                    x1_ref, h2_ref, comb_ref, *, bb, tt):
    D = x_ref.shape[-1]
    tm = bb * tt
    CW = 512
    chunks = [slice(c * CW, (c + 1) * CW) for c in range(D // CW)]

    ssq = jnp.zeros((tm, 1), F32)
    for cols in chunks:
        mix = (jnp.dot(attn_ref[...], wo_ref[0:WIDTH, cols], preferred_element_type=F32)
               + jnp.dot(ret_ref[...], wo_ref[WIDTH:2 * WIDTH, cols], preferred_element_type=F32))
        x1 = x_ref[:, cols].reshape(bb, tt, CW) + gate_ref[:, :, cols] * mix.reshape(bb, tt, CW)
        x1 = x1.reshape(tm, CW)
        x1_ref[:, cols] = x1
        ssq = ssq + jnp.sum(x1 * x1, axis=-1, keepdims=True)
    rinv = lax.rsqrt(ssq * (1.0 / D) + EPS)

    logits = jnp.zeros((tm, LANES), F32)
    for cols in chunks:
        h2 = (x1_ref[:, cols] * rinv * g2_ref[:, cols]).reshape(bb, tt, CW)
        h2 = (h2 * (1.0 + scale_ref[:, :, cols]) + shift_ref[:, :, cols]).reshape(tm, CW)
        h_hi = h2.astype(BF16)
        h2_ref[:, cols] = h_hi
        h_lo = (h2 - h_hi.astype(F32)).astype(BF16)
        logits = (logits + jnp.dot(h_hi, wrh_ref[cols, :], preferred_element_type=F32)
                  + jnp.dot(h_lo, wrh_ref[cols, :], preferred_element_type=F32)
                  + jnp.dot(h_hi, wrl_ref[cols, :], preferred_element_type=F32))
    lane = lax.broadcasted_iota(I32, logits.shape, 1)
    ninf = -jnp.inf
    rmax = lambda v: jnp.max(v, axis=-1, keepdims=True)
    rsum = lambda v: jnp.sum(v, axis=-1, keepdims=True)
    first = lambda m: jnp.min(jnp.where(m, lane, LANES), axis=-1, keepdims=True)
    gl = jnp.where(lane < N_GROUPS, logits, ninf)
    gmax = rmax(gl)
    g_top = 1.0 / rsum(jnp.exp(gl - gmax))
    g_idx = first(gl == gmax)
    emask = (lane >= N_GROUPS) & (lane < N_GROUPS + N_EXPERTS) & (((lane - N_GROUPS) >> 2) == g_idx)
    el = jnp.where(emask, logits, ninf)
    emax = rmax(el)
    esum = rsum(jnp.exp(el - emax))
    i1 = first(el == emax)
    el2 = jnp.where(lane == i1, ninf, el)
    emax2 = rmax(el2)
    i2 = first(el2 == emax2)
    p1 = 1.0 / esum
    p2 = jnp.exp(emax2 - emax) / esum
    den = p1 + p2
    comb_ref[...] = jnp.where(lane == i1, g_top * (p1 / den),
                              jnp.where(lane == i2, g_top * (p2 / den), 0.0))


def _outproj(attn, ret, x2d, gate1, shift2, scale2, g2, wo, wr_hi, wr_lo, *, bb, tt):
    N, D = x2d.shape
    tm = bb * tt
    tok = lambda w: pl.BlockSpec((tm, w), lambda i: (i, 0))
    mod = pl.BlockSpec((bb, 1, D), lambda i: (i if bb > 1 else 0, 0, 0))
    full = lambda a: pl.BlockSpec(a.shape, lambda i: (0,) * a.ndim)
    return pl.pallas_call(
        functools.partial(_outproj_kernel, bb=bb, tt=tt),
        grid=(N // tm,),
        in_specs=[tok(WIDTH), tok(WIDTH), tok(D), mod, mod, mod, full(g2), full(wo), full(wr_hi), full(wr_lo)],
        out_specs=[tok(D), tok(D), tok(LANES)],
        out_shape=[jax.ShapeDtypeStruct((N, D), F32), jax.ShapeDtypeStruct((N, D), BF16),
                   jax.ShapeDtypeStruct((N, LANES), F32)],
        compiler_params=_params(("parallel",)),
        name="outproj_router",
    )(attn, ret, x2d, gate1, shift2, scale2, g2, wo, wr_hi, wr_lo)


def _moe_kernel(h_ref, comb_ref, x1_ref, gate_ref, wg_ref, wu_ref, wd_ref, y_ref, acc_scr, *, bb, tt):
    e = pl.program_id(1)
    D = x1_ref.shape[-1]

    @pl.when(e == 0)
    def _():
        acc_scr[...] = jnp.zeros(acc_scr.shape, F32)

    h = h_ref[...]
    a = jnp.dot(h, wg_ref[0], preferred_element_type=F32)
    b = jnp.dot(h, wu_ref[0], preferred_element_type=F32)
    mid = (_silu(a) * b).astype(BF16)
    out = jnp.dot(mid, wd_ref[0], preferred_element_type=F32)
    lane = lax.broadcasted_iota(I32, comb_ref.shape, 1)
    w = jnp.sum(jnp.where(lane == e + N_GROUPS, comb_ref[...], 0.0), axis=-1, keepdims=True)
    acc_scr[...] += w * out

    @pl.when(e == N_EXPERTS - 1)
    def _():
        y = x1_ref[...].reshape(bb, tt, D) + gate_ref[...] * acc_scr[...].reshape(bb, tt, D)
        y_ref[...] = y.reshape(bb * tt, D)


def _moe(h2, comb, x1, gate2, wg, wu, wd, *, bb, tt):
    N, D = x1.shape
    tm = bb * tt
    tok = lambda w: pl.BlockSpec((tm, w), lambda i, e: (i, 0))
    mod = pl.BlockSpec((bb, 1, D), lambda i, e: (i if bb > 1 else 0, 0, 0))
    return pl.pallas_call(
        functools.partial(_moe_kernel, bb=bb, tt=tt),
        grid=(N // tm, N_EXPERTS),
        in_specs=[tok(D), tok(LANES), tok(D), mod,
                  pl.BlockSpec((1, D, D_EXPERT), lambda i, e: (e, 0, 0)),
                  pl.BlockSpec((1, D, D_EXPERT), lambda i, e: (e, 0, 0)),
                  pl.BlockSpec((1, D_EXPERT, D), lambda i, e: (e, 0, 0))],
        out_specs=tok(D),
        out_shape=jax.ShapeDtypeStruct((N, D), F32),
        scratch_shapes=[pltpu.VMEM((tm, D), F32)],
        compiler_params=_params(("parallel", "arbitrary")),
        name="moe",
    )(h2, comb, x1, gate2, wg, wu, wd)


def _moe_routed_kernel(h_ref, comb_ref, x1_hbm, gate_ref, wg_ref, wu_ref, wd_ref, y_ref,
                       rank_scr, rankT_scr, gT_scr, sem):
    i = pl.program_id(0)
    e = pl.program_id(1)
    tm, D = y_ref.shape
    R, KP = MOE_ROWS, MOE_SCATTER_K

    @pl.when(e == 0)
    def _():
        residual = pltpu.make_async_copy(x1_hbm.at[pl.ds(i * tm, tm), :], y_ref, sem)
        residual.start()
        comb = comb_ref[...]
        used = comb != 0.0
        ti = lax.broadcasted_iota(I32, (tm, tm), 0)
        tj = lax.broadcasted_iota(I32, (tm, tm), 1)
        earlier = jnp.where(tj < ti, 1.0, 0.0).astype(BF16)
        rank = jnp.dot(earlier, jnp.where(used, 1.0, 0.0).astype(BF16), preferred_element_type=F32)
        rank = jnp.where(used, rank, -1.0)
        rank_scr[...] = rank
        rankT_scr[...] = rank.T
        gT_scr[...] = comb.T
        residual.wait()

    lane_e = e + N_GROUPS
    lane = lax.broadcasted_iota(I32, (tm, LANES), 1)
    r_col = jnp.sum(jnp.where(lane == lane_e, rank_scr[...], 0.0), axis=-1, keepdims=True)
    r_row = rankT_scr[pl.ds(lane_e, 1), :]
    g_row = gT_scr[pl.ds(lane_e, 1), :]
    n_e = jnp.sum(jnp.where(r_row >= 0.0, 1, 0))
    gate2 = gate_ref[0]
    CW = 512

    def chunk(c, carry):
        base = jnp.asarray(c * R, F32)
        ridx = base + lax.broadcasted_iota(I32, (R, tm), 0).astype(F32)
        pm = r_row == ridx
        x = jnp.dot(jnp.where(pm, 1.0, 0.0).astype(BF16), h_ref[...], preferred_element_type=F32).astype(BF16)
        a = jnp.dot(x, wg_ref[0], preferred_element_type=F32)
        b = jnp.dot(x, wu_ref[0], preferred_element_type=F32)
        mid = (_silu(a) * b).astype(BF16)
        y = jnp.dot(mid, wd_ref[0], preferred_element_type=F32)
        g_r = jnp.sum(jnp.where(pm, g_row, 0.0), axis=-1, keepdims=True)
        ys = (y * g_r * gate2).astype(BF16)
        ys = jnp.concatenate([ys, jnp.zeros((KP - R, D), BF16)], axis=0)
        cidx = lax.broadcasted_iota(I32, (tm, KP), 1)
        sm = (r_col == base + cidx.astype(F32)) & (cidx < R)
        s = jnp.where(sm, 1.0, 0.0).astype(BF16)
        for c0 in range(0, D, CW):
            y_ref[:, c0:c0 + CW] += jnp.dot(s, ys[:, c0:c0 + CW], preferred_element_type=F32)
        return carry

    lax.fori_loop(0, (n_e + R - 1) // R, chunk, 0)


def _moe_routed(h2, comb, x1, gate2, wg, wu, wd):
    N, D = x1.shape
    tm = MOE_TOKENS
    assert gate2.shape[0] == 1 and N % tm == 0
    tok = lambda w: pl.BlockSpec((tm, w), lambda i, e: (i, 0))
    return pl.pallas_call(
        _moe_routed_kernel,
        grid=(N // tm, N_EXPERTS),
        in_specs=[tok(D), tok(LANES), pl.BlockSpec(memory_space=pl.ANY),
                  pl.BlockSpec((1, 1, D), lambda i, e: (0, 0, 0)),
                  pl.BlockSpec((1, D, D_EXPERT), lambda i, e: (e, 0, 0)),
                  pl.BlockSpec((1, D, D_EXPERT), lambda i, e: (e, 0, 0)),
                  pl.BlockSpec((1, D_EXPERT, D), lambda i, e: (e, 0, 0))],
        out_specs=tok(D),
        out_shape=jax.ShapeDtypeStruct((N, D), F32),
        scratch_shapes=[pltpu.VMEM((tm, LANES), F32), pltpu.VMEM((LANES, tm), F32),
                        pltpu.VMEM((LANES, tm), F32), pltpu.SemaphoreType.DMA(())],
        compiler_params=_params(("parallel", "arbitrary")),
        name="moe_routed",
    )(h2, comb, x1, gate2, wg, wu, wd)


def _layer(x, mod, past, W, *, bb, tt, TQ, TK, KC, ret_chunk):
    B, T, D = x.shape
    N = B * T
    x2d = x.reshape(N, D)
    shift1, scale1, gate1, shift2, scale2, gate2 = [m.reshape(B, 1, D) for m in jnp.split(mod, 6, axis=-1)]
    z = _inproj(x2d, shift1, scale1, W["g1"], W["w_in"], W["qg"], W["kg"], bb=bb, tt=tt)
    vT_new = z["avT"].reshape(N_HEADS, VT_ROWS, B, T)
    vT_new = vT_new.reshape(1, N_HEADS, VT_ROWS, T) if B == 1 else vT_new.transpose(2, 0, 1, 3)

    if past is None:
        state0 = jnp.zeros((B, N_HEADS, HEAD_DIM, HEAD_DIM), F32)
        P, Lreal = 0, T
        topk = min(TOPK_MAX, T // 4)
        kb = z["akb"].reshape(B, T, WIDTH)
        vT = vT_new
        ikb = z["ikb"].reshape(B, T, IDX_DIM)
        Tq = T
        pad_q = lambda a: a.reshape(B, T, a.shape[-1])
    else:
        ck, cv, cki, state0 = past
        P = ck.shape[1]
        Lreal = P + T
        topk = min(TOPK_MAX, Lreal // 4)
        Lp = -(-Lreal // TK) * TK
        kb, vT, ikb = _assemble_cache(ck, cv, cki, z["akb"].reshape(B, T, WIDTH), vT_new,
                                      z["ikb"].reshape(B, T, IDX_DIM), Lp)
        Tq = -(-T // TQ) * TQ
        pad_q = lambda a: jnp.pad(a.reshape(B, T, a.shape[-1]), ((0, 0), (0, Tq - T), (0, 0)))

    iq = pad_q(z["iq"])
    iwT = pad_q(z["iw"]).reshape(B, Tq // TQ, TQ, IDX_HEADS).transpose(0, 1, 3, 2)
    keysT, tn = _index_select(iq, ikb, iwT, TQ=TQ, TK=TK, P=P, Lreal=Lreal, topk=topk)
    attn = _attention(pad_q(z["aq"]), kb, vT, keysT, tn, TQ=TQ, TK=TK, KC=KC, P=P, Lreal=Lreal)
    attn = attn[:, :T].reshape(N, WIDTH)

    ret, ret_state = _retention(z["rq"], z["rk"], z["rv"], z["rg"], state0, B=B, T=T, C=ret_chunk)
    x1, h2, comb = _outproj(attn, ret, x2d, gate1, shift2, scale2, W["g2"], W["w_out"], W["w_r_hi"],
                            W["w_r_lo"], bb=bb, tt=tt)
    if B == 1 and N % MOE_TOKENS == 0:
        y = _moe_routed(h2, comb, x1, gate2, W["wg"], W["wu"], W["wd"])
    else:
        y = _moe(h2, comb, x1, gate2, W["wg"], W["wu"], W["wd"], bb=bb, tt=tt)
    return (y.reshape(B, T, D), z["ak"].reshape(B, T, N_HEADS, HEAD_DIM),
            z["av"].reshape(B, T, N_HEADS, HEAD_DIM), z["ik"].reshape(B, T, IDX_DIM), ret_state)


def kernel(x_prompt, x_sample, cache_k, cache_v, cache_kidx, state_ret, c_prompt, c_sample, w_ada, b_ada,
           norm1_g, norm2_g, w_in, q_norm_g, k_norm_g, w_out, w_group, w_router, w_gate_e, w_up_e, w_down_e):
    depth = w_ada.shape[0]
    D = D_MODEL
    Bp, Bs = x_prompt.shape[0], x_sample.shape[0]
    Ts = x_sample.shape[1]
    xp, xs = x_prompt, x_sample
    outs_p, outs_s = [], []
    for l in range(depth):
        w_r = jnp.concatenate(
            [w_group[l], jnp.moveaxis(w_router[l], 0, 1).reshape(D, N_EXPERTS),
             jnp.zeros((D, LANES - N_GROUPS - N_EXPERTS), F32)], axis=1)
        w_r_hi = w_r.astype(BF16)
        W = {
            "g1": norm1_g[l].reshape(1, D), "g2": norm2_g[l].reshape(1, D),
            "qg": q_norm_g[l].reshape(1, HEAD_DIM), "kg": k_norm_g[l].reshape(1, HEAD_DIM),
            "w_in": _pad_cast_w_in(w_in[l]),
            "w_out": w_out[l].astype(BF16),
            "w_r_hi": w_r_hi, "w_r_lo": (w_r - w_r_hi.astype(F32)).astype(BF16),
            "wg": w_gate_e[l].astype(BF16), "wu": w_up_e[l].astype(BF16), "wd": w_down_e[l].astype(BF16),
        }
        rows = Bp + Bs
        rows_p = -(-rows // 8) * 8
        c_all = jnp.concatenate([c_prompt, c_sample, jnp.zeros((rows_p - rows, D), F32)], axis=0)
        mod = _adaln_mod(c_all, w_ada[l], b_ada[l])
        past = (cache_k[l], cache_v[l], cache_kidx[l], state_ret[l])
        xp, kp, vp, kip, sp = _layer(xp, mod[:Bp], None, W, bb=1, tt=512, TQ=256, TK=512, KC=256,
                                     ret_chunk=256)
        xs, kn, vn, kin, sn = _layer(xs, mod[Bp:rows], past, W, bb=512 // Ts, tt=Ts, TQ=128, TK=384, KC=384,
                                     ret_chunk=Ts)
        outs_p.append((kp, vp, kip, sp))
        outs_s.append((kn, vn, kin, sn))
    st = lambda xs_, i: jnp.stack([o[i] for o in xs_])
    return (xp, xs, st(outs_p, 0), st(outs_p, 1), st(outs_p, 2), st(outs_p, 3),
            st(outs_s, 0), st(outs_s, 1), st(outs_s, 2), st(outs_s, 3))
```

```python
import functools
import math

import numpy as np
import jax
import jax.numpy as jnp
from jax import lax
from jax.experimental import pallas as pl
from jax.experimental.pallas import tpu as pltpu

F32 = jnp.float32
BF16 = jnp.bfloat16
I32 = jnp.int32

D_MODEL = 2048
CHUNK = 64
N_HEADS = 8
HEAD_DIM = 128
WIDTH = N_HEADS * HEAD_DIM
IDX_HEADS = 8
IDX_DIM = 64
TOPK_MAX = 256
ATTN_SCALE = HEAD_DIM ** -0.5
IDX_SCALE = IDX_DIM ** -0.5
IDX_W_SCALE = IDX_HEADS ** -0.5
RET_K_SCALE = HEAD_DIM ** -0.5
N_GROUPS = 4
EXPERTS_PER_GROUP = 4
N_EXPERTS = N_GROUPS * EXPERTS_PER_GROUP
D_EXPERT = 512
EPS = 1e-6

OFF_IQ = 3 * WIDTH
IDX_COLS = IDX_HEADS * IDX_DIM + IDX_DIM + IDX_HEADS
OFF_RQ = OFF_IQ + IDX_COLS
N_COL_GROUPS = 8

LANES = 128
INT_MIN = -(2 ** 31)
TAKE_ALL = 2 ** 30
MASKED_DIST = 1e33
VMEM_LIMIT = 56 * 1024 * 1024
MOE_TOKENS = 1024
MOE_ROWS = 160
MOE_SCATTER_K = 256

LOG2E = 1.4426950408889634
VT_ROWS = HEAD_DIM + 16
ALIBI_SLOPES = [float(2.0 ** (-8.0 * (h + 1) / N_HEADS)) for h in range(N_HEADS)]
LOG_GAMMA = [float(np.log1p(-(2.0 ** (-5.0 - h)))) for h in range(N_HEADS)]


def _params(sem):
    return pltpu.CompilerParams(dimension_semantics=sem, vmem_limit_bytes=VMEM_LIMIT)


def _silu(x):
    return x * jax.nn.sigmoid(x)


def _mod_kernel(c_ref, w_ref, b_ref, o_ref):
    s = _silu(c_ref[...])
    o_ref[...] = jnp.dot(s, w_ref[...], precision=lax.Precision.HIGHEST,
                         preferred_element_type=F32) + b_ref[...]


def _adaln_mod(c, w_ada, b_ada):
    R, D = c.shape
    N = w_ada.shape[1]
    tn = 1024
    return pl.pallas_call(
        _mod_kernel,
        grid=(N // tn,),
        in_specs=[pl.BlockSpec((R, D), lambda j: (0, 0)),
                  pl.BlockSpec((D, tn), lambda j: (0, j)),
                  pl.BlockSpec((1, tn), lambda j: (0, j))],
        out_specs=pl.BlockSpec((R, tn), lambda j: (0, j)),
        out_shape=jax.ShapeDtypeStruct((R, N), F32),
        compiler_params=_params(("arbitrary",)),
        name="adaln_mod",
    )(c, w_ada, b_ada.reshape(1, N))


def _w_in_kernel(w_ref, o_ref):
    rows = w_ref.shape[0]
    pad = o_ref.shape[1] - w_ref.shape[1]
    o_ref[:, 0:OFF_RQ] = w_ref[:, 0:OFF_RQ].astype(BF16)
    o_ref[:, OFF_RQ:OFF_RQ + pad] = jnp.zeros((rows, pad), BF16)
    o_ref[:, OFF_RQ + pad:] = w_ref[:, OFF_RQ:].astype(BF16)


def _pad_cast_w_in(w):
    D, n_in = w.shape
    n_out = N_COL_GROUPS * WIDTH
    rows = 128
    return pl.pallas_call(
        _w_in_kernel,
        grid=(D // rows,),
        in_specs=[pl.BlockSpec((rows, n_in), lambda i: (i, 0))],
        out_specs=pl.BlockSpec((rows, n_out), lambda i: (i, 0)),
        out_shape=jax.ShapeDtypeStruct((D, n_out), BF16),
        compiler_params=_params(("parallel",)),
        name="pad_cast_w_in",
    )(w)


def _inproj_kernel(x_ref, shift_ref, scale_ref, g1_ref, w_ref, qg_ref, kg_ref,
                   aq_ref, ak_ref, akb_ref, av_ref, avT_ref, iq_ref, ik_ref, ikb_ref, iw_ref,
                   rq_ref, rk_ref, rv_ref, rg_ref, h_scr, *, bb, tt):
    j = pl.program_id(1)
    D = x_ref.shape[-1]

    @pl.when(j == 0)
    def _():
        x = x_ref[...]
        ms = jnp.mean(x * x, axis=-1, keepdims=True)
        y = x * lax.rsqrt(ms + EPS) * g1_ref[...]
        y = y.reshape(bb, tt, D) * (1.0 + scale_ref[...]) + shift_ref[...]
        h_scr[...] = y.reshape(bb * tt, D).astype(BF16)

    CW = 2 * HEAD_DIM
    chunks = [slice(c * CW, (c + 1) * CW) for c in range(WIDTH // CW)]

    def proj(cols):
        return jnp.dot(h_scr[...], w_ref[:, cols], preferred_element_type=F32)

    def head_rms(zc, g_ref):
        outs = []
        for hh in range(CW // HEAD_DIM):
            zh = zc[:, hh * HEAD_DIM:(hh + 1) * HEAD_DIM]
            ms = jnp.mean(zh * zh, axis=-1, keepdims=True)
            outs.append(zh * lax.rsqrt(ms + EPS) * g_ref[...])
        return jnp.concatenate(outs, axis=1)

    @pl.when(j == 0)
    def _():
        for cols in chunks:
            v = head_rms(proj(cols), qg_ref)
            aq_ref[:, cols] = (v * (ATTN_SCALE * LOG2E)).astype(BF16)

    def store_heads(ref, c, v):
        tm = v.shape[0]
        for hh in range(CW // HEAD_DIM):
            h = c * (CW // HEAD_DIM) + hh
            ref[pl.ds(h, tm, stride=N_HEADS), :] = v[:, hh * HEAD_DIM:(hh + 1) * HEAD_DIM]

    @pl.when(j == 1)
    def _():
        for c, cols in enumerate(chunks):
            v = head_rms(proj(cols), kg_ref)
            akb_ref[:, cols] = v.astype(BF16)
            store_heads(ak_ref, c, v)

    @pl.when(j == 2)
    def _():
        for c, cols in enumerate(chunks):
            zc = proj(cols)
            store_heads(av_ref, c, zc)
            zT = zc.T
            for hh in range(CW // HEAD_DIM):
                h = c * (CW // HEAD_DIM) + hh
                avT_ref[h, 0:HEAD_DIM, :] = zT[hh * HEAD_DIM:(hh + 1) * HEAD_DIM, :].astype(BF16)
                avT_ref[h, HEAD_DIM:VT_ROWS, :] = jnp.ones((VT_ROWS - HEAD_DIM, zT.shape[1]), BF16)

    @pl.when(j == 3)
    def _():
        nq = IDX_HEADS * IDX_DIM
        for cols in chunks[:nq // CW]:
            iq_ref[:, cols] = (proj(cols) * IDX_SCALE).astype(BF16)
        zc = proj(chunks[nq // CW])
        ik = zc[:, 0:IDX_DIM]
        ik_ref[...] = ik
        ikb_ref[...] = ik.astype(BF16)
        iw_ref[...] = zc[:, IDX_DIM:IDX_DIM + IDX_HEADS] * IDX_W_SCALE

    @pl.when(j == 4)
    def _():
        for cols in chunks:
            rq_ref[:, cols] = proj(cols).astype(BF16)

    @pl.when(j == 5)
    def _():
        for cols in chunks:
            rk_ref[:, cols] = (proj(cols) * RET_K_SCALE).astype(BF16)

    @pl.when(j == 6)
    def _():
        for cols in chunks:
            rv_ref[:, cols] = proj(cols).astype(BF16)

    @pl.when(j == 7)
    def _():
        for cols in chunks:
            rg_ref[:, cols] = proj(cols)


def _inproj(x2d, shift, scale, g1, w_p, qg, kg, *, bb, tt):
    N, D = x2d.shape
    tm = bb * tt
    nI = N // tm
    tok = lambda w: pl.BlockSpec((tm, w), lambda i, j: (i, 0))
    mod = pl.BlockSpec((bb, 1, D), lambda i, j: (i if bb > 1 else 0, 0, 0))
    row = lambda w: pl.BlockSpec((1, w), lambda i, j: (0, 0))
    outs = [("aq", WIDTH, BF16), ("ak", "heads", F32), ("akb", WIDTH, BF16), ("av", "heads", F32),
            ("avT", "vT", BF16), ("iq", IDX_HEADS * IDX_DIM, BF16), ("ik", IDX_DIM, F32),
            ("ikb", IDX_DIM, BF16), ("iw", IDX_HEADS, F32), ("rq", WIDTH, BF16), ("rk", WIDTH, BF16),
            ("rv", WIDTH, BF16), ("rg", WIDTH, F32)]
    special = {"heads": (pl.BlockSpec((tm * N_HEADS, HEAD_DIM), lambda i, j: (i, 0)), (N * N_HEADS, HEAD_DIM)),
               "vT": (pl.BlockSpec((N_HEADS, VT_ROWS, tm), lambda i, j: (0, 0, i)), (N_HEADS, VT_ROWS, N))}
    out_specs = [special[w][0] if w in special else tok(w) for _, w, _ in outs]
    out_shape = [jax.ShapeDtypeStruct(special[w][1] if w in special else (N, w), dt) for _, w, dt in outs]
    res = pl.pallas_call(
        functools.partial(_inproj_kernel, bb=bb, tt=tt),
        grid=(nI, N_COL_GROUPS),
        in_specs=[tok(D), mod, mod, row(D),
                  pl.BlockSpec((D, WIDTH), lambda i, j: (0, j)), row(HEAD_DIM), row(HEAD_DIM)],
        out_specs=out_specs,
        out_shape=out_shape,
        scratch_shapes=[pltpu.VMEM((tm, D), BF16)],
        compiler_params=_params(("parallel", "arbitrary")),
        name="inproj",
    )(x2d, shift, scale, g1, w_p, qg, kg)
    return {name: r for (name, _, _), r in zip(outs, res)}


def _ret_kernel(q_ref, k_ref, v_ref, g_ref, s0_ref, o_ref, sn_ref, st_scr, *, C):
    c = pl.program_id(1)

    @pl.when(c == 0)
    def _():
        st_scr[...] = s0_ref[0]

    pi = lax.broadcasted_iota(I32, (C, C), 0)
    pj = lax.broadcasted_iota(I32, (C, C), 1)
    diff = (pi - pj).astype(F32)
    causal = pi >= pj
    pos = lax.broadcasted_iota(I32, (C, HEAD_DIM), 0).astype(F32)
    for h in range(N_HEADS):
        lg = LOG_GAMMA[h]
        sl = slice(h * HEAD_DIM, (h + 1) * HEAD_DIM)
        q = q_ref[:, sl]
        k = k_ref[:, sl]
        v = v_ref[:, sl]
        decay = jnp.where(causal, jnp.exp(lg * jnp.maximum(diff, 0.0)), 0.0)
        s = lax.dot_general(q, k, (((1,), (1,)), ((), ())), preferred_element_type=F32) * decay
        o = jnp.dot(s.astype(BF16), v, preferred_element_type=F32)
        st = st_scr[h]
        cross = jnp.exp(lg * (pos + 1.0))
        o = o + jnp.dot(q, st.astype(BF16), preferred_element_type=F32) * cross
        kdec = jnp.exp(lg * (C - 1.0 - pos))
        kd = (k.astype(F32) * kdec).astype(BF16)
        st_new = math.exp(lg * C) * st + lax.dot_general(
            kd, v, (((0,), (0,)), ((), ())), preferred_element_type=F32)
        st_scr[h] = st_new
        ms = jnp.mean(o * o, axis=-1, keepdims=True)
        o_ref[:, sl] = (o * lax.rsqrt(ms + EPS) * _silu(g_ref[:, sl])).astype(BF16)

    @pl.when(c == pl.num_programs(1) - 1)
    def _():
        sn_ref[0] = st_scr[...]


def _retention(rq, rk, rv, rg, state0, *, B, T, C):
    nC = T // C
    tok = pl.BlockSpec((C, WIDTH), lambda b, c: (b * nC + c, 0))
    st = pl.BlockSpec((1, N_HEADS, HEAD_DIM, HEAD_DIM), lambda b, c: (b, 0, 0, 0))
    return pl.pallas_call(
        functools.partial(_ret_kernel, C=C),
        grid=(B, nC),
        in_specs=[tok, tok, tok, tok, st],
        out_specs=[tok, st],
        out_shape=[jax.ShapeDtypeStruct((B * T, WIDTH), BF16),
                   jax.ShapeDtypeStruct((B, N_HEADS, HEAD_DIM, HEAD_DIM), F32)],
        scratch_shapes=[pltpu.VMEM((N_HEADS, HEAD_DIM, HEAD_DIM), F32)],
        compiler_params=_params(("parallel", "arbitrary")),
        name="retention",
    )(rq, rk, rv, rg, state0)


def _cache_kernel(ck_ref, cv_ref, cki_ref, kn_ref, vn_ref, in_ref, kb_ref, vT_ref, ikb_ref):
    P = cki_ref.shape[1]
    T = kn_ref.shape[1]
    Lp = kb_ref.shape[1]
    for h in range(N_HEADS):
        sl = slice(h * HEAD_DIM, (h + 1) * HEAD_DIM)
        kb_ref[0, 0:P, sl] = ck_ref[0, pl.ds(h, P, stride=N_HEADS), :].astype(BF16)
        vT_ref[0, h, 0:HEAD_DIM, 0:P] = cv_ref[0, pl.ds(h, P, stride=N_HEADS), :].T.astype(BF16)
        vT_ref[0, h, HEAD_DIM:VT_ROWS, 0:P] = jnp.ones((VT_ROWS - HEAD_DIM, P), BF16)
        vT_ref[0, h, :, P:Lp] = jnp.concatenate(
            [vn_ref[0, h], jnp.zeros((VT_ROWS, Lp - P - T), BF16)], axis=1)
    kb_ref[0, P:P + T, :] = kn_ref[0]
    kb_ref[0, P + T:Lp, :] = jnp.zeros((Lp - P - T, WIDTH), BF16)
    ikb_ref[0, 0:P, :] = cki_ref[0].astype(BF16)
    ikb_ref[0, P:P + T, :] = in_ref[0]
    ikb_ref[0, P + T:Lp, :] = jnp.zeros((Lp - P - T, IDX_DIM), BF16)


def _assemble_cache(ck, cv, cki, k_new, vT_new, ik_new, Lp):
    B, P = ck.shape[:2]
    T = k_new.shape[1]
    lead = lambda *blk: pl.BlockSpec((1,) + blk, lambda b: (b,) + (0,) * len(blk))
    return pl.pallas_call(
        _cache_kernel,
        grid=(B,),
        in_specs=[lead(P * N_HEADS, HEAD_DIM), lead(P * N_HEADS, HEAD_DIM), lead(P, IDX_DIM),
                  lead(T, WIDTH), lead(N_HEADS, VT_ROWS, T), lead(T, IDX_DIM)],
        out_specs=[lead(Lp, WIDTH), lead(N_HEADS, VT_ROWS, Lp), lead(Lp, IDX_DIM)],
        out_shape=[jax.ShapeDtypeStruct((B, Lp, WIDTH), BF16),
                   jax.ShapeDtypeStruct((B, N_HEADS, VT_ROWS, Lp), BF16),
                   jax.ShapeDtypeStruct((B, Lp, IDX_DIM), BF16)],
        compiler_params=_params(("parallel",)),
        name="assemble_cache",
    )(ck.reshape(B, P * N_HEADS, HEAD_DIM), cv.reshape(B, P * N_HEADS, HEAD_DIM), cki, k_new, vT_new, ik_new)


def _key_limit(q_first, q_count, P, Lreal):
    return jnp.minimum(Lreal, ((P + q_first + q_count - 1) // CHUNK + 1) * CHUNK)


def _sorting_program(n, keep):
    pairs = []

    def merge(lo, m, r):
        step = r * 2
        if step < m:
            merge(lo, m, step)
            merge(lo + r, m, step)
            pairs.extend((i, i + r) for i in range(lo + r, lo + m - r, step))
        else:
            pairs.append((lo, lo + r))

    def sort(lo, m):
        if m > 1:
            sort(lo, m // 2)
            sort(lo + m // 2, m // 2)
            merge(lo, m, 1)

    sort(0, n)
    need, prog = set(range(keep)), []
    for i, j in reversed(pairs):
        if i in need or j in need:
            prog.append((i, j, i in need, j in need))
            need.update((i, j))
    return prog[::-1]


SORT_GROUP = 16
SORT_LEVELS = 8
SORT_MIN_KEYS = 6400
SORT_PROGRAM = _sorting_program(SORT_GROUP, SORT_LEVELS)


def _idx_kernel(iq_ref, ik_ref, iwT_ref, keys_ref, tn_ref, lev_scr, *, TQ, TK, P, Lreal, topk):
    qb = pl.program_id(1)
    nkt = (_key_limit(qb * TQ, TQ, P, Lreal) + TK - 1) // TK
    tq = P + qb * TQ + lax.broadcasted_iota(I32, (1, TQ), 1)
    visible = jnp.minimum(Lreal, ((tq >> 6) + 1) << 6)
    lrow = lax.broadcasted_iota(I32, (TK, TQ), 0)
    LR = TK // SORT_GROUP
    use_levels = nkt * TK >= SORT_MIN_KEYS

    def tile_body(kt, carry):
        r0 = pl.multiple_of(kt * TK, TK)
        ik = ik_ref[0, pl.ds(r0, TK), :]
        acc = jnp.zeros((TK, TQ), F32)
        for h in range(IDX_HEADS):
            qh = iq_ref[0, :, h * IDX_DIM:(h + 1) * IDX_DIM]
            r = lax.dot_general(ik, qh, (((1,), (1,)), ((), ())), preferred_element_type=F32)
            acc = acc + jnp.maximum(r, 0.0) * iwT_ref[0, 0, h:h + 1, :]
        bits = pltpu.bitcast(acc, I32)
        key = bits ^ ((bits >> 31) & 0x7FFFFFFF)
        key = jnp.where(lrow < visible - r0, key, INT_MIN)
        keys_ref[0, 0, pl.ds(r0, TK), :] = key

        x = key.reshape(TK // (SORT_GROUP * 8), SORT_GROUP, 8, TQ)
        w = [x[:, j] for j in range(SORT_GROUP)]
        for i, j, need_max, need_min in SORT_PROGRAM:
            hi, lo = jnp.maximum(w[i], w[j]), jnp.minimum(w[i], w[j])
            if need_max:
                w[i] = hi
            if need_min:
                w[j] = lo
        for v in range(SORT_LEVELS):
            lev_scr[v, pl.ds(pl.multiple_of(kt * LR, LR), LR), :] = w[v].reshape(LR, TQ)
        return carry

    lax.fori_loop(0, nkt, tile_body, 0)

    def fill_body(kt, carry):
        keys_ref[0, 0, pl.ds(pl.multiple_of(kt * TK, TK), TK), :] = jnp.full((TK, TQ), INT_MIN, I32)
        return carry

    lax.fori_loop(nkt, keys_ref.shape[2] // TK, fill_body, 0)

    SUB = 64

    def count(pred):
        def body(kt, acc):
            r0 = pl.multiple_of(kt * TK, TK)
            for s in range(TK // SUB):
                blk = keys_ref[0, 0, pl.ds(r0 + s * SUB, SUB), :]
                acc = acc + pred(blk).astype(I32)
            return acc
        acc = lax.fori_loop(0, nkt, body, jnp.zeros((SUB, TQ), I32))
        return jnp.sum(acc, axis=0, keepdims=True)

    def count_levels(pred, levels):
        def body(kt, acc):
            r0 = pl.multiple_of(kt * LR, LR)
            for v in levels:
                acc = acc + pred(lev_scr[v, pl.ds(r0, LR), :]).astype(I32)
            return acc
        acc = lax.fori_loop(0, nkt, body, jnp.zeros((LR, TQ), I32))
        return jnp.sum(acc, axis=0, keepdims=True)

    def search(count_fn):
        def pass_body(b, carry):
            t_u, cnt_t = carry
            cand_u = t_u | lax.shift_left(jnp.int32(1), jnp.asarray(31 - b, I32))
            cand_s = cand_u ^ INT_MIN
            cnt = count_fn(lambda blk: blk >= cand_s)
            take = cnt >= topk
            return jnp.where(take, cand_u, t_u), jnp.where(take, cnt, cnt_t)
        return lax.fori_loop(0, 32, pass_body,
                             (jnp.zeros((1, TQ), I32), jnp.full((1, TQ), TAKE_ALL, I32)))

    scanned = range(SORT_LEVELS - 1)

    def search_levels():
        t_u, cnt_t = search(lambda pred: count_levels(pred, scanned))
        t_s = t_u ^ INT_MIN
        missed = count_levels(lambda blk: blk >= t_s, [SORT_LEVELS - 1])
        return t_u, cnt_t, jnp.max(missed)

    t_u, cnt_t, missed = lax.cond(
        use_levels, search_levels,
        lambda: (jnp.zeros((1, TQ), I32), jnp.full((1, TQ), TAKE_ALL, I32), jnp.int32(1)))
    full_scan = missed > 0
    t_u, cnt_t = lax.cond(full_scan, lambda: search(count), lambda: (t_u, cnt_t))
    t_s = t_u ^ INT_MIN
    ties = (cnt_t > topk) & (t_s != INT_MIN)
    any_ties = jnp.max(ties.astype(I32)) > 0
    count_gt = lambda: lax.cond(full_scan, lambda: count(lambda blk: blk > t_s),
                                lambda: count_levels(lambda blk: blk > t_s, scanned))
    cnt_gt = lax.cond(any_ties, count_gt, lambda: jnp.zeros((1, TQ), I32))
    n_take = jnp.where(ties, topk - cnt_gt, TAKE_ALL)
    row = lax.broadcasted_iota(I32, (8, TQ), 0)
    tn_ref[0, 0] = jnp.where(row == 0, t_s, n_take)


def _index_select(iq, ikb, iwT, *, TQ, TK, P, Lreal, topk):
    nB, Lp, _ = ikb.shape
    nQ = iwT.shape[1]
    return pl.pallas_call(
        functools.partial(_idx_kernel, TQ=TQ, TK=TK, P=P, Lreal=Lreal, topk=topk),
        grid=(nB, nQ),
        in_specs=[pl.BlockSpec((1, TQ, IDX_HEADS * IDX_DIM), lambda b, q: (b, q, 0)),
                  pl.BlockSpec((1, Lp, IDX_DIM), lambda b, q: (b, 0, 0)),
                  pl.BlockSpec((1, 1, IDX_HEADS, TQ), lambda b, q: (b, q, 0, 0))],
        out_specs=[pl.BlockSpec((1, 1, Lp, TQ), lambda b, q: (b, q, 0, 0)),
                   pl.BlockSpec((1, 1, 8, TQ), lambda b, q: (b, q, 0, 0))],
        out_shape=[jax.ShapeDtypeStruct((nB, nQ, Lp, TQ), I32),
                   jax.ShapeDtypeStruct((nB, nQ, 8, TQ), I32)],
        scratch_shapes=[pltpu.VMEM((SORT_LEVELS, Lp // SORT_GROUP, TQ), I32)],
        compiler_params=_params(("parallel", "arbitrary")),
        name="index_select",
    )(iq, ikb, iwT)


def _attn_kernel(qbs_ref, kts_ref, last_ref, q_ref, k_ref, vT_ref, keys_ref, tn_ref, o_ref,
                 qT_scr, ndm_scr, a_scr, m_scr, l_scr, acc_scr, tie_scr, *, TQ, TK, KC, P, Lreal):
    s_id = pl.program_id(1)
    qb = qbs_ref[s_id]
    kt = kts_ref[s_id]
    is_last = last_ref[s_id] == 1
    heads = [slice(h * HEAD_DIM, (h + 1) * HEAD_DIM) for h in range(N_HEADS)]

    @pl.when(kt == 0)
    def _():
        m_scr[...] = jnp.full(m_scr.shape, -jnp.inf, F32)
        l_scr[...] = jnp.zeros(l_scr.shape, F32)
        acc_scr[...] = jnp.zeros(acc_scr.shape, F32)
        tie_scr[...] = jnp.zeros(tie_scr.shape, F32)
        for sl in heads:
            qT_scr[sl, :] = q_ref[0, :, sl].astype(F32).T.astype(BF16)

    keys = keys_ref[0, 0]
    thr = tn_ref[0, 0, 0:1, :]
    ntk = tn_ref[0, 0, 1:2, :]
    l = kt * TK + lax.broadcasted_iota(I32, (TK, TQ), 0)
    tq = P + qb * TQ + lax.broadcasted_iota(I32, (TK, TQ), 1)
    has_ties = jnp.max(jnp.where(ntk == TAKE_ALL, 0, 1)) > 0
    no_ties = jnp.logical_not(has_ties)

    @pl.when(jnp.logical_and(no_ties, jnp.logical_not(is_last)))
    def _():
        ndm_scr[...] = jnp.where(keys >= thr, (l - tq).astype(F32), -MASKED_DIST)

    @pl.when(jnp.logical_and(no_ties, is_last))
    def _():
        allowed = (l < Lreal) & ((l >> 6) <= (tq >> 6))
        ndm_scr[...] = jnp.where((keys >= thr) & allowed, -jnp.abs(tq - l).astype(F32), -MASKED_DIST)

    @pl.when(has_ties)
    def _():
        allowed = (l < Lreal) & ((l >> 6) <= (tq >> 6))
        eq = keys == thr
        li = lax.broadcasted_iota(I32, (TK, TK), 0)
        lj = lax.broadcasted_iota(I32, (TK, TK), 1)
        lower = jnp.where(lj <= li, 1.0, 0.0).astype(BF16)
        prefix = jnp.dot(lower, jnp.where(eq, 1.0, 0.0).astype(BF16), preferred_element_type=F32)
        rank = tie_scr[0:1, :] + prefix
        sel = ((keys > thr) | (eq & (rank <= ntk.astype(F32)))) & allowed
        ndm_scr[...] = jnp.where(sel, -jnp.abs(tq - l).astype(F32), -MASKED_DIST)
        tie_scr[0:1, :] = tie_scr[0:1, :] + prefix[TK - 1:TK, :]

    chunks = [slice(c * KC, (c + 1) * KC) for c in range(TK // KC)]

    def stage_a(h, rows, mx):
        s = jnp.dot(k_ref[0, rows, heads[h]], qT_scr[heads[h], :], preferred_element_type=F32)
        a = s + (ALIBI_SLOPES[h] * LOG2E) * ndm_scr[rows, :]
        a_scr[h % 2, rows, :] = a
        cm = jnp.max(a, axis=0, keepdims=True)
        return cm if mx is None else jnp.maximum(mx, cm)

    def stage_b(h, rows, m_new, pv):
        p = jnp.exp2(a_scr[h % 2, rows, :] - m_new).astype(BF16)
        d = jnp.dot(vT_ref[0, h, :, rows], p, preferred_element_type=F32)
        return d if pv is None else pv + d

    m_all = m_scr[...]
    l_all = l_scr[...]
    m_out, l_out = [], []
    mx = None
    for rows in chunks:
        mx = stage_a(0, rows, mx)
    for h in range(N_HEADS):
        m_prev = m_all[h:h + 1]
        m_new = jnp.maximum(m_prev, mx)
        alpha = jnp.exp2(m_prev - m_new)
        mx, pv = None, None
        for rows in chunks:
            if h + 1 < N_HEADS:
                mx = stage_a(h + 1, rows, mx)
            pv = stage_b(h, rows, m_new, pv)
        acc_scr[heads[h], :] = alpha * acc_scr[heads[h], :] + pv[0:HEAD_DIM]
        l_out.append(alpha * l_all[h:h + 1] + pv[HEAD_DIM:HEAD_DIM + 1])
        m_out.append(m_new)
    m_scr[...] = jnp.concatenate(m_out, axis=0)
    l_scr[...] = jnp.concatenate(l_out, axis=0)

    @pl.when(is_last)
    def _():
        for h in range(N_HEADS):
            o_ref[0, :, heads[h]] = (acc_scr[heads[h], :] / l_scr[h:h + 1, :]).T.astype(BF16)


def _attention(aq, kb, vT, keysT, tn, *, TQ, TK, KC, P, Lreal):
    nB, Tq, _ = aq.shape
    nQ = Tq // TQ
    qbs, kts, last = [], [], []
    for qb in range(nQ):
        limit = min(Lreal, ((P + qb * TQ + TQ - 1) // CHUNK + 1) * CHUNK)
        n = -(-limit // TK)
        assert (n - 1) * TK <= P + qb * TQ
        qbs += [qb] * n
        kts += list(range(n))
        last += [0] * (n - 1) + [1]
    steps = len(qbs)
    qmap = lambda b, s, qbs, kts, last: (b, qbs[s], 0)
    grid_spec = pltpu.PrefetchScalarGridSpec(
        num_scalar_prefetch=3,
        grid=(nB, steps),
        in_specs=[pl.BlockSpec((1, TQ, WIDTH), qmap),
                  pl.BlockSpec((1, TK, WIDTH), lambda b, s, qbs, kts, last: (b, kts[s], 0)),
                  pl.BlockSpec((1, N_HEADS, VT_ROWS, TK), lambda b, s, qbs, kts, last: (b, 0, 0, kts[s])),
                  pl.BlockSpec((1, 1, TK, TQ), lambda b, s, qbs, kts, last: (b, qbs[s], kts[s], 0)),
                  pl.BlockSpec((1, 1, 8, TQ), lambda b, s, qbs, kts, last: (b, qbs[s], 0, 0))],
        out_specs=pl.BlockSpec((1, TQ, WIDTH), qmap),
        scratch_shapes=[pltpu.VMEM((WIDTH, TQ), BF16),
                        pltpu.VMEM((TK, TQ), F32),
                        pltpu.VMEM((2, TK, TQ), F32),
                        pltpu.VMEM((N_HEADS, TQ), F32),
                        pltpu.VMEM((N_HEADS, TQ), F32),
                        pltpu.VMEM((WIDTH, TQ), F32),
                        pltpu.VMEM((8, TQ), F32)])
    arr = lambda v: jnp.asarray(np.array(v, np.int32))
    return pl.pallas_call(
        functools.partial(_attn_kernel, TQ=TQ, TK=TK, KC=KC, P=P, Lreal=Lreal),
        grid_spec=grid_spec,
        out_shape=jax.ShapeDtypeStruct((nB, Tq, WIDTH), BF16),
        compiler_params=_params(("parallel", "arbitrary")),
        name="sparse_attention",
    )(arr(qbs), arr(kts), arr(last), aq, kb, vT, keysT, tn)


def _outproj_kernel(attn_ref, ret_ref, x_ref, gate_ref, shift_ref, scale_ref, g2_ref, wo_ref, wrh_ref, wrl_ref,
                    x1_ref, h2_ref, comb_ref, *, bb, tt):
    D = x_ref.shape[-1]
    tm = bb * tt
    CW = 512
    chunks = [slice(c * CW, (c + 1) * CW) for c in range(D // CW)]

    ssq = jnp.zeros((tm, 1), F32)
    for cols in chunks:
        mix = (jnp.dot(attn_ref[...], wo_ref[0:WIDTH, cols], preferred_element_type=F32)
               + jnp.dot(ret_ref[...], wo_ref[WIDTH:2 * WIDTH, cols], preferred_element_type=F32))
        x1 = x_ref[:, cols].reshape(bb, tt, CW) + gate_ref[:, :, cols] * mix.reshape(bb, tt, CW)
        x1 = x1.reshape(tm, CW)
        x1_ref[:, cols] = x1
        ssq = ssq + jnp.sum(x1 * x1, axis=-1, keepdims=True)
    rinv = lax.rsqrt(ssq * (1.0 / D) + EPS)

    logits = jnp.zeros((tm, LANES), F32)
    for cols in chunks:
        h2 = (x1_ref[:, cols] * rinv * g2_ref[:, cols]).reshape(bb, tt, CW)
        h2 = (h2 * (1.0 + scale_ref[:, :, cols]) + shift_ref[:, :, cols]).reshape(tm, CW)
        h_hi = h2.astype(BF16)
        h2_ref[:, cols] = h_hi
        h_lo = (h2 - h_hi.astype(F32)).astype(BF16)
        logits = (logits + jnp.dot(h_hi, wrh_ref[cols, :], preferred_element_type=F32)
                  + jnp.dot(h_lo, wrh_ref[cols, :], preferred_element_type=F32)
                  + jnp.dot(h_hi, wrl_ref[cols, :], preferred_element_type=F32))
    lane = lax.broadcasted_iota(I32, logits.shape, 1)
    ninf = -jnp.inf
    rmax = lambda v: jnp.max(v, axis=-1, keepdims=True)
    rsum = lambda v: jnp.sum(v, axis=-1, keepdims=True)
    first = lambda m: jnp.min(jnp.where(m, lane, LANES), axis=-1, keepdims=True)
    gl = jnp.where(lane < N_GROUPS, logits, ninf)
    gmax = rmax(gl)
    g_top = 1.0 / rsum(jnp.exp(gl - gmax))
    g_idx = first(gl == gmax)
    emask = (lane >= N_GROUPS) & (lane < N_GROUPS + N_EXPERTS) & (((lane - N_GROUPS) >> 2) == g_idx)
    el = jnp.where(emask, logits, ninf)
    emax = rmax(el)
    esum = rsum(jnp.exp(el - emax))
    i1 = first(el == emax)
    el2 = jnp.where(lane == i1, ninf, el)
    emax2 = rmax(el2)
    i2 = first(el2 == emax2)
    p1 = 1.0 / esum
    p2 = jnp.exp(emax2 - emax) / esum
    den = p1 + p2
    comb_ref[...] = jnp.where(lane == i1, g_top * (p1 / den),
                              jnp.where(lane == i2, g_top * (p2 / den), 0.0))


def _outproj(attn, ret, x2d, gate1, shift2, scale2, g2, wo, wr_hi, wr_lo, *, bb, tt):
    N, D = x2d.shape
    tm = bb * tt
    tok = lambda w: pl.BlockSpec((tm, w), lambda i: (i, 0))
    mod = pl.BlockSpec((bb, 1, D), lambda i: (i if bb > 1 else 0, 0, 0))
    full = lambda a: pl.BlockSpec(a.shape, lambda i: (0,) * a.ndim)
    return pl.pallas_call(
        functools.partial(_outproj_kernel, bb=bb, tt=tt),
        grid=(N // tm,),
        in_specs=[tok(WIDTH), tok(WIDTH), tok(D), mod, mod, mod, full(g2), full(wo), full(wr_hi), full(wr_lo)],
        out_specs=[tok(D), tok(D), tok(LANES)],
        out_shape=[jax.ShapeDtypeStruct((N, D), F32), jax.ShapeDtypeStruct((N, D), BF16),
                   jax.ShapeDtypeStruct((N, LANES), F32)],
        compiler_params=_params(("parallel",)),
        name="outproj_router",
    )(attn, ret, x2d, gate1, shift2, scale2, g2, wo, wr_hi, wr_lo)


def _moe_kernel(h_ref, comb_ref, x1_ref, gate_ref, wg_ref, wu_ref, wd_ref, y_ref, acc_scr, *, bb, tt):
    e = pl.program_id(1)
    D = x1_ref.shape[-1]

    @pl.when(e == 0)
    def _():
        acc_scr[...] = jnp.zeros(acc_scr.shape, F32)

    h = h_ref[...]
    a = jnp.dot(h, wg_ref[0], preferred_element_type=F32)
    b = jnp.dot(h, wu_ref[0], preferred_element_type=F32)
    mid = (_silu(a) * b).astype(BF16)
    out = jnp.dot(mid, wd_ref[0], preferred_element_type=F32)
    lane = lax.broadcasted_iota(I32, comb_ref.shape, 1)
    w = jnp.sum(jnp.where(lane == e + N_GROUPS, comb_ref[...], 0.0), axis=-1, keepdims=True)
    acc_scr[...] += w * out

    @pl.when(e == N_EXPERTS - 1)
    def _():
        y = x1_ref[...].reshape(bb, tt, D) + gate_ref[...] * acc_scr[...].reshape(bb, tt, D)
        y_ref[...] = y.reshape(bb * tt, D)


def _moe(h2, comb, x1, gate2, wg, wu, wd, *, bb, tt):
    N, D = x1.shape
    tm = bb * tt
    tok = lambda w: pl.BlockSpec((tm, w), lambda i, e: (i, 0))
    mod = pl.BlockSpec((bb, 1, D), lambda i, e: (i if bb > 1 else 0, 0, 0))
    return pl.pallas_call(
        functools.partial(_moe_kernel, bb=bb, tt=tt),
        grid=(N // tm, N_EXPERTS),
        in_specs=[tok(D), tok(LANES), tok(D), mod,
                  pl.BlockSpec((1, D, D_EXPERT), lambda i, e: (e, 0, 0)),
                  pl.BlockSpec((1, D, D_EXPERT), lambda i, e: (e, 0, 0)),
                  pl.BlockSpec((1, D_EXPERT, D), lambda i, e: (e, 0, 0))],
        out_specs=tok(D),
        out_shape=jax.ShapeDtypeStruct((N, D), F32),
        scratch_shapes=[pltpu.VMEM((tm, D), F32)],
        compiler_params=_params(("parallel", "arbitrary")),
        name="moe",
    )(h2, comb, x1, gate2, wg, wu, wd)


def _moe_routed_kernel(h_ref, comb_ref, x1_hbm, gate_ref, wg_ref, wu_ref, wd_ref, y_ref,
                       rank_scr, rankT_scr, gT_scr, sem):
    i = pl.program_id(0)
    e = pl.program_id(1)
    tm, D = y_ref.shape
    R, KP = MOE_ROWS, MOE_SCATTER_K

    @pl.when(e == 0)
    def _():
        residual = pltpu.make_async_copy(x1_hbm.at[pl.ds(i * tm, tm), :], y_ref, sem)
        residual.start()
        comb = comb_ref[...]
        used = comb != 0.0
        ti = lax.broadcasted_iota(I32, (tm, tm), 0)
        tj = lax.broadcasted_iota(I32, (tm, tm), 1)
        earlier = jnp.where(tj < ti, 1.0, 0.0).astype(BF16)
        rank = jnp.dot(earlier, jnp.where(used, 1.0, 0.0).astype(BF16), preferred_element_type=F32)
        rank = jnp.where(used, rank, -1.0)
        rank_scr[...] = rank
        rankT_scr[...] = rank.T
        gT_scr[...] = comb.T
        residual.wait()

    lane_e = e + N_GROUPS
    lane = lax.broadcasted_iota(I32, (tm, LANES), 1)
    r_col = jnp.sum(jnp.where(lane == lane_e, rank_scr[...], 0.0), axis=-1, keepdims=True)
    r_row = rankT_scr[pl.ds(lane_e, 1), :]
    g_row = gT_scr[pl.ds(lane_e, 1), :]
    n_e = jnp.sum(jnp.where(r_row >= 0.0, 1, 0))
    gate2 = gate_ref[0]
    CW = 512

    def chunk(c, carry):
        base = jnp.asarray(c * R, F32)
        ridx = base + lax.broadcasted_iota(I32, (R, tm), 0).astype(F32)
        pm = r_row == ridx
        x = jnp.dot(jnp.where(pm, 1.0, 0.0).astype(BF16), h_ref[...], preferred_element_type=F32).astype(BF16)
        a = jnp.dot(x, wg_ref[0], preferred_element_type=F32)
        b = jnp.dot(x, wu_ref[0], preferred_element_type=F32)
        mid = (_silu(a) * b).astype(BF16)
        y = jnp.dot(mid, wd_ref[0], preferred_element_type=F32)
        g_r = jnp.sum(jnp.where(pm, g_row, 0.0), axis=-1, keepdims=True)
        ys = (y * g_r * gate2).astype(BF16)
        ys = jnp.concatenate([ys, jnp.zeros((KP - R, D), BF16)], axis=0)
        cidx = lax.broadcasted_iota(I32, (tm, KP), 1)
        sm = (r_col == base + cidx.astype(F32)) & (cidx < R)
        s = jnp.where(sm, 1.0, 0.0).astype(BF16)
        for c0 in range(0, D, CW):
            y_ref[:, c0:c0 + CW] += jnp.dot(s, ys[:, c0:c0 + CW], preferred_element_type=F32)
        return carry

    lax.fori_loop(0, (n_e + R - 1) // R, chunk, 0)


def _moe_routed(h2, comb, x1, gate2, wg, wu, wd):
    N, D = x1.shape
    tm = MOE_TOKENS
    assert gate2.shape[0] == 1 and N % tm == 0
    tok = lambda w: pl.BlockSpec((tm, w), lambda i, e: (i, 0))
    return pl.pallas_call(
        _moe_routed_kernel,
        grid=(N // tm, N_EXPERTS),
        in_specs=[tok(D), tok(LANES), pl.BlockSpec(memory_space=pl.ANY),
                  pl.BlockSpec((1, 1, D), lambda i, e: (0, 0, 0)),
                  pl.BlockSpec((1, D, D_EXPERT), lambda i, e: (e, 0, 0)),
                  pl.BlockSpec((1, D, D_EXPERT), lambda i, e: (e, 0, 0)),
                  pl.BlockSpec((1, D_EXPERT, D), lambda i, e: (e, 0, 0))],
        out_specs=tok(D),
        out_shape=jax.ShapeDtypeStruct((N, D), F32),
        scratch_shapes=[pltpu.VMEM((tm, LANES), F32), pltpu.VMEM((LANES, tm), F32),
                        pltpu.VMEM((LANES, tm), F32), pltpu.SemaphoreType.DMA(())],
        compiler_params=_params(("parallel", "arbitrary")),
        name="moe_routed",
    )(h2, comb, x1, gate2, wg, wu, wd)


def _layer(x, mod, past, W, *, bb, tt, TQ, TK, KC, ret_chunk):
    B, T, D = x.shape
    N = B * T
    x2d = x.reshape(N, D)
    shift1, scale1, gate1, shift2, scale2, gate2 = [m.reshape(B, 1, D) for m in jnp.split(mod, 6, axis=-1)]
    z = _inproj(x2d, shift1, scale1, W["g1"], W["w_in"], W["qg"], W["kg"], bb=bb, tt=tt)
    vT_new = z["avT"].reshape(N_HEADS, VT_ROWS, B, T)
    vT_new = vT_new.reshape(1, N_HEADS, VT_ROWS, T) if B == 1 else vT_new.transpose(2, 0, 1, 3)

    if past is None:
        state0 = jnp.zeros((B, N_HEADS, HEAD_DIM, HEAD_DIM), F32)
        P, Lreal = 0, T
        topk = min(TOPK_MAX, T // 4)
        kb = z["akb"].reshape(B, T, WIDTH)
        vT = vT_new
        ikb = z["ikb"].reshape(B, T, IDX_DIM)
        Tq = T
        pad_q = lambda a: a.reshape(B, T, a.shape[-1])
    else:
        ck, cv, cki, state0 = past
        P = ck.shape[1]
        Lreal = P + T
        topk = min(TOPK_MAX, Lreal // 4)
        Lp = -(-Lreal // TK) * TK
        kb, vT, ikb = _assemble_cache(ck, cv, cki, z["akb"].reshape(B, T, WIDTH), vT_new,
                                      z["ikb"].reshape(B, T, IDX_DIM), Lp)
        Tq = -(-T // TQ) * TQ
        pad_q = lambda a: jnp.pad(a.reshape(B, T, a.shape[-1]), ((0, 0), (0, Tq - T), (0, 0)))

    iq = pad_q(z["iq"])
    iwT = pad_q(z["iw"]).reshape(B, Tq // TQ, TQ, IDX_HEADS).transpose(0, 1, 3, 2)
    keysT, tn = _index_select(iq, ikb, iwT, TQ=TQ, TK=TK, P=P, Lreal=Lreal, topk=topk)
    attn = _attention(pad_q(z["aq"]), kb, vT, keysT, tn, TQ=TQ, TK=TK, KC=KC, P=P, Lreal=Lreal)
    attn = attn[:, :T].reshape(N, WIDTH)

    ret, ret_state = _retention(z["rq"], z["rk"], z["rv"], z["rg"], state0, B=B, T=T, C=ret_chunk)
    x1, h2, comb = _outproj(attn, ret, x2d, gate1, shift2, scale2, W["g2"], W["w_out"], W["w_r_hi"],
                            W["w_r_lo"], bb=bb, tt=tt)
    if B == 1 and N % MOE_TOKENS == 0:
        y = _moe_routed(h2, comb, x1, gate2, W["wg"], W["wu"], W["wd"])
    else:
        y = _moe(h2, comb, x1, gate2, W["wg"], W["wu"], W["wd"], bb=bb, tt=tt)
    return (y.reshape(B, T, D), z["ak"].reshape(B, T, N_HEADS, HEAD_DIM),
            z["av"].reshape(B, T, N_HEADS, HEAD_DIM), z["ik"].reshape(B, T, IDX_DIM), ret_state)


def kernel(x_prompt, x_sample, cache_k, cache_v, cache_kidx, state_ret, c_prompt, c_sample, w_ada, b_ada,
           norm1_g, norm2_g, w_in, q_norm_g, k_norm_g, w_out, w_group, w_router, w_gate_e, w_up_e, w_down_e):
    depth = w_ada.shape[0]
    D = D_MODEL
    Bp, Bs = x_prompt.shape[0], x_sample.shape[0]
    Ts = x_sample.shape[1]
    xp, xs = x_prompt, x_sample
    outs_p, outs_s = [], []
    for l in range(depth):
        w_r = jnp.concatenate(
            [w_group[l], jnp.moveaxis(w_router[l], 0, 1).reshape(D, N_EXPERTS),
             jnp.zeros((D, LANES - N_GROUPS - N_EXPERTS), F32)], axis=1)
        w_r_hi = w_r.astype(BF16)
        W = {
            "g1": norm1_g[l].reshape(1, D), "g2": norm2_g[l].reshape(1, D),
            "qg": q_norm_g[l].reshape(1, HEAD_DIM), "kg": k_norm_g[l].reshape(1, HEAD_DIM),
            "w_in": _pad_cast_w_in(w_in[l]),
            "w_out": w_out[l].astype(BF16),
            "w_r_hi": w_r_hi, "w_r_lo": (w_r - w_r_hi.astype(F32)).astype(BF16),
            "wg": w_gate_e[l].astype(BF16), "wu": w_up_e[l].astype(BF16), "wd": w_down_e[l].astype(BF16),
        }
        rows = Bp + Bs
        rows_p = -(-rows // 8) * 8
        c_all = jnp.concatenate([c_prompt, c_sample, jnp.zeros((rows_p - rows, D), F32)], axis=0)
        mod = _adaln_mod(c_all, w_ada[l], b_ada[l])
        past = (cache_k[l], cache_v[l], cache_kidx[l], state_ret[l])
        xp, kp, vp, kip, sp = _layer(xp, mod[:Bp], None, W, bb=1, tt=512, TQ=256, TK=512, KC=256,
                                     ret_chunk=256)
        xs, kn, vn, kin, sn = _layer(xs, mod[Bp:rows], past, W, bb=512 // Ts, tt=Ts, TQ=128, TK=384, KC=384,
                                     ret_chunk=Ts)
        outs_p.append((kp, vp, kip, sp))
        outs_s.append((kn, vn, kin, sn))
    st = lambda xs_, i: jnp.stack([o[i] for o in xs_])
    return (xp, xs, st(outs_p, 0), st(outs_p, 1), st(outs_p, 2), st(outs_p, 3),
            st(outs_s, 0), st(outs_s, 1), st(outs_s, 2), st(outs_s, 3))
```

```python
import functools
import math

import numpy as np
import jax
import jax.numpy as jnp
from jax import lax
from jax.experimental import pallas as pl
from jax.experimental.pallas import tpu as pltpu

F32 = jnp.float32
BF16 = jnp.bfloat16
I32 = jnp.int32

D_MODEL = 2048
CHUNK = 64
N_HEADS = 8
HEAD_DIM = 128
WIDTH = N_HEADS * HEAD_DIM
IDX_HEADS = 8
IDX_DIM = 64
TOPK_MAX = 256
ATTN_SCALE = HEAD_DIM ** -0.5
IDX_SCALE = IDX_DIM ** -0.5
IDX_W_SCALE = IDX_HEADS ** -0.5
RET_K_SCALE = HEAD_DIM ** -0.5
N_GROUPS = 4
EXPERTS_PER_GROUP = 4
N_EXPERTS = N_GROUPS * EXPERTS_PER_GROUP
D_EXPERT = 512
EPS = 1e-6

OFF_IQ = 3 * WIDTH
IDX_COLS = IDX_HEADS * IDX_DIM + IDX_DIM + IDX_HEADS
OFF_RQ = OFF_IQ + IDX_COLS
N_COL_GROUPS = 8

LANES = 128
INT_MIN = -(2 ** 31)
TAKE_ALL = 2 ** 30
MASKED_DIST = 1e33
VMEM_LIMIT = 56 * 1024 * 1024
ATTN_SCORE_SLOTS = 3
MOE_TOKENS = 1024
MOE_ROWS = 160
MOE_SCATTER_K = 256

LOG2E = 1.4426950408889634
VT_ROWS = HEAD_DIM + 16
ALIBI_SLOPES = [float(2.0 ** (-8.0 * (h + 1) / N_HEADS)) for h in range(N_HEADS)]
LOG_GAMMA = [float(np.log1p(-(2.0 ** (-5.0 - h)))) for h in range(N_HEADS)]


def _params(sem):
    return pltpu.CompilerParams(dimension_semantics=sem, vmem_limit_bytes=VMEM_LIMIT)


def _silu(x):
    return x * jax.nn.sigmoid(x)


def _mod_kernel(c_ref, w_ref, b_ref, o_ref):
    s = _silu(c_ref[...])
    o_ref[...] = jnp.dot(s, w_ref[...], precision=lax.Precision.HIGHEST,
                         preferred_element_type=F32) + b_ref[...]


def _adaln_mod(c, w_ada, b_ada):
    R, D = c.shape
    N = w_ada.shape[1]
    tn = 1024
    return pl.pallas_call(
        _mod_kernel,
        grid=(N // tn,),
        in_specs=[pl.BlockSpec((R, D), lambda j: (0, 0)),
                  pl.BlockSpec((D, tn), lambda j: (0, j)),
                  pl.BlockSpec((1, tn), lambda j: (0, j))],
        out_specs=pl.BlockSpec((R, tn), lambda j: (0, j)),
        out_shape=jax.ShapeDtypeStruct((R, N), F32),
        compiler_params=_params(("arbitrary",)),
        name="adaln_mod",
    )(c, w_ada, b_ada.reshape(1, N))


def _w_in_kernel(w_ref, o_ref):
    rows = w_ref.shape[0]
    pad = o_ref.shape[1] - w_ref.shape[1]
    o_ref[:, 0:OFF_RQ] = w_ref[:, 0:OFF_RQ].astype(BF16)
    o_ref[:, OFF_RQ:OFF_RQ + pad] = jnp.zeros((rows, pad), BF16)
    o_ref[:, OFF_RQ + pad:] = w_ref[:, OFF_RQ:].astype(BF16)


def _pad_cast_w_in(w):
    D, n_in = w.shape
    n_out = N_COL_GROUPS * WIDTH
    rows = 128
    return pl.pallas_call(
        _w_in_kernel,
        grid=(D // rows,),
        in_specs=[pl.BlockSpec((rows, n_in), lambda i: (i, 0))],
        out_specs=pl.BlockSpec((rows, n_out), lambda i: (i, 0)),
        out_shape=jax.ShapeDtypeStruct((D, n_out), BF16),
        compiler_params=_params(("parallel",)),
        name="pad_cast_w_in",
    )(w)


def _inproj_kernel(x_ref, shift_ref, scale_ref, g1_ref, w_ref, qg_ref, kg_ref,
                   aq_ref, ak_ref, akb_ref, av_ref, avT_ref, iq_ref, ik_ref, ikb_ref, iw_ref,
                   rq_ref, rk_ref, rv_ref, rg_ref, h_scr, *, bb, tt):
    j = pl.program_id(1)
    D = x_ref.shape[-1]

    @pl.when(j == 0)
    def _():
        x = x_ref[...]
        ms = jnp.mean(x * x, axis=-1, keepdims=True)
        y = x * lax.rsqrt(ms + EPS) * g1_ref[...]
        y = y.reshape(bb, tt, D) * (1.0 + scale_ref[...]) + shift_ref[...]
        h_scr[...] = y.reshape(bb * tt, D).astype(BF16)

    CW = 2 * HEAD_DIM
    chunks = [slice(c * CW, (c + 1) * CW) for c in range(WIDTH // CW)]

    def proj(cols):
        return jnp.dot(h_scr[...], w_ref[:, cols], preferred_element_type=F32)

    def head_rms(zc, g_ref):
        outs = []
        for hh in range(CW // HEAD_DIM):
            zh = zc[:, hh * HEAD_DIM:(hh + 1) * HEAD_DIM]
            ms = jnp.mean(zh * zh, axis=-1, keepdims=True)
            outs.append(zh * lax.rsqrt(ms + EPS) * g_ref[...])
        return jnp.concatenate(outs, axis=1)

    @pl.when(j == 0)
    def _():
        for cols in chunks:
            v = head_rms(proj(cols), qg_ref)
            aq_ref[:, cols] = (v * (ATTN_SCALE * LOG2E)).astype(BF16)

    def store_heads(ref, c, v):
        tm = v.shape[0]
        for hh in range(CW // HEAD_DIM):
            h = c * (CW // HEAD_DIM) + hh
            ref[pl.ds(h, tm, stride=N_HEADS), :] = v[:, hh * HEAD_DIM:(hh + 1) * HEAD_DIM]

    @pl.when(j == 1)
    def _():
        for c, cols in enumerate(chunks):
            v = head_rms(proj(cols), kg_ref)
            akb_ref[:, cols] = v.astype(BF16)
            store_heads(ak_ref, c, v)

    @pl.when(j == 2)
    def _():
        for c, cols in enumerate(chunks):
            zc = proj(cols)
            store_heads(av_ref, c, zc)
            zT = zc.T
            for hh in range(CW // HEAD_DIM):
                h = c * (CW // HEAD_DIM) + hh
                avT_ref[h, 0:HEAD_DIM, :] = zT[hh * HEAD_DIM:(hh + 1) * HEAD_DIM, :].astype(BF16)
                avT_ref[h, HEAD_DIM:VT_ROWS, :] = jnp.ones((VT_ROWS - HEAD_DIM, zT.shape[1]), BF16)

    @pl.when(j == 3)
    def _():
        nq = IDX_HEADS * IDX_DIM
        for cols in chunks[:nq // CW]:
            iq_ref[:, cols] = (proj(cols) * IDX_SCALE).astype(BF16)
        zc = proj(chunks[nq // CW])
        ik = zc[:, 0:IDX_DIM]
        ik_ref[...] = ik
        ikb_ref[...] = ik.astype(BF16)
        iw_ref[...] = zc[:, IDX_DIM:IDX_DIM + IDX_HEADS] * IDX_W_SCALE

    @pl.when(j == 4)
    def _():
        for cols in chunks:
            rq_ref[:, cols] = proj(cols).astype(BF16)

    @pl.when(j == 5)
    def _():
        for cols in chunks:
            rk_ref[:, cols] = (proj(cols) * RET_K_SCALE).astype(BF16)

    @pl.when(j == 6)
    def _():
        for cols in chunks:
            rv_ref[:, cols] = proj(cols).astype(BF16)

    @pl.when(j == 7)
    def _():
        for cols in chunks:
            rg_ref[:, cols] = proj(cols)


def _inproj(x2d, shift, scale, g1, w_p, qg, kg, *, bb, tt):
    N, D = x2d.shape
    tm = bb * tt
    nI = N // tm
    tok = lambda w: pl.BlockSpec((tm, w), lambda i, j: (i, 0))
    mod = pl.BlockSpec((bb, 1, D), lambda i, j: (i if bb > 1 else 0, 0, 0))
    row = lambda w: pl.BlockSpec((1, w), lambda i, j: (0, 0))
    outs = [("aq", WIDTH, BF16), ("ak", "heads", F32), ("akb", WIDTH, BF16), ("av", "heads", F32),
            ("avT", "vT", BF16), ("iq", IDX_HEADS * IDX_DIM, BF16), ("ik", IDX_DIM, F32),
            ("ikb", IDX_DIM, BF16), ("iw", IDX_HEADS, F32), ("rq", WIDTH, BF16), ("rk", WIDTH, BF16),
            ("rv", WIDTH, BF16), ("rg", WIDTH, F32)]
    special = {"heads": (pl.BlockSpec((tm * N_HEADS, HEAD_DIM), lambda i, j: (i, 0)), (N * N_HEADS, HEAD_DIM)),
               "vT": (pl.BlockSpec((N_HEADS, VT_ROWS, tm), lambda i, j: (0, 0, i)), (N_HEADS, VT_ROWS, N))}
    out_specs = [special[w][0] if w in special else tok(w) for _, w, _ in outs]
    out_shape = [jax.ShapeDtypeStruct(special[w][1] if w in special else (N, w), dt) for _, w, dt in outs]
    res = pl.pallas_call(
        functools.partial(_inproj_kernel, bb=bb, tt=tt),
        grid=(nI, N_COL_GROUPS),
        in_specs=[tok(D), mod, mod, row(D),
                  pl.BlockSpec((D, WIDTH), lambda i, j: (0, j)), row(HEAD_DIM), row(HEAD_DIM)],
        out_specs=out_specs,
        out_shape=out_shape,
        scratch_shapes=[pltpu.VMEM((tm, D), BF16)],
        compiler_params=_params(("parallel", "arbitrary")),
        name="inproj",
    )(x2d, shift, scale, g1, w_p, qg, kg)
    return {name: r for (name, _, _), r in zip(outs, res)}


def _ret_kernel(q_ref, k_ref, v_ref, g_ref, s0_ref, o_ref, sn_ref, st_scr, *, C):
    c = pl.program_id(1)

    @pl.when(c == 0)
    def _():
        st_scr[...] = s0_ref[0]

    pi = lax.broadcasted_iota(I32, (C, C), 0)
    pj = lax.broadcasted_iota(I32, (C, C), 1)
    diff = (pi - pj).astype(F32)
    causal = pi >= pj
    pos = lax.broadcasted_iota(I32, (C, HEAD_DIM), 0).astype(F32)
    for h in range(N_HEADS):
        lg = LOG_GAMMA[h]
        sl = slice(h * HEAD_DIM, (h + 1) * HEAD_DIM)
        q = q_ref[:, sl]
        k = k_ref[:, sl]
        v = v_ref[:, sl]
        decay = jnp.where(causal, jnp.exp(lg * jnp.maximum(diff, 0.0)), 0.0)
        s = lax.dot_general(q, k, (((1,), (1,)), ((), ())), preferred_element_type=F32) * decay
        o = jnp.dot(s.astype(BF16), v, preferred_element_type=F32)
        st = st_scr[h]
        cross = jnp.exp(lg * (pos + 1.0))
        o = o + jnp.dot(q, st.astype(BF16), preferred_element_type=F32) * cross
        kdec = jnp.exp(lg * (C - 1.0 - pos))
        kd = (k.astype(F32) * kdec).astype(BF16)
        st_new = math.exp(lg * C) * st + lax.dot_general(
            kd, v, (((0,), (0,)), ((), ())), preferred_element_type=F32)
        st_scr[h] = st_new
        ms = jnp.mean(o * o, axis=-1, keepdims=True)
        o_ref[:, sl] = (o * lax.rsqrt(ms + EPS) * _silu(g_ref[:, sl])).astype(BF16)

    @pl.when(c == pl.num_programs(1) - 1)
    def _():
        sn_ref[0] = st_scr[...]


def _retention(rq, rk, rv, rg, state0, *, B, T, C):
    nC = T // C
    tok = pl.BlockSpec((C, WIDTH), lambda b, c: (b * nC + c, 0))
    st = pl.BlockSpec((1, N_HEADS, HEAD_DIM, HEAD_DIM), lambda b, c: (b, 0, 0, 0))
    return pl.pallas_call(
        functools.partial(_ret_kernel, C=C),
        grid=(B, nC),
        in_specs=[tok, tok, tok, tok, st],
        out_specs=[tok, st],
        out_shape=[jax.ShapeDtypeStruct((B * T, WIDTH), BF16),
                   jax.ShapeDtypeStruct((B, N_HEADS, HEAD_DIM, HEAD_DIM), F32)],
        scratch_shapes=[pltpu.VMEM((N_HEADS, HEAD_DIM, HEAD_DIM), F32)],
        compiler_params=_params(("parallel", "arbitrary")),
        name="retention",
    )(rq, rk, rv, rg, state0)


def _cache_kernel(ck_ref, cv_ref, cki_ref, kn_ref, vn_ref, in_ref, kb_ref, vT_ref, ikb_ref):
    P = cki_ref.shape[1]
    T = kn_ref.shape[1]
    Lp = kb_ref.shape[1]
    for h in range(N_HEADS):
        sl = slice(h * HEAD_DIM, (h + 1) * HEAD_DIM)
        kb_ref[0, 0:P, sl] = ck_ref[0, pl.ds(h, P, stride=N_HEADS), :].astype(BF16)
        vT_ref[0, h, 0:HEAD_DIM, 0:P] = cv_ref[0, pl.ds(h, P, stride=N_HEADS), :].T.astype(BF16)
        vT_ref[0, h, HEAD_DIM:VT_ROWS, 0:P] = jnp.ones((VT_ROWS - HEAD_DIM, P), BF16)
        vT_ref[0, h, :, P:Lp] = jnp.concatenate(
            [vn_ref[0, h], jnp.zeros((VT_ROWS, Lp - P - T), BF16)], axis=1)
    kb_ref[0, P:P + T, :] = kn_ref[0]
    kb_ref[0, P + T:Lp, :] = jnp.zeros((Lp - P - T, WIDTH), BF16)
    ikb_ref[0, 0:P, :] = cki_ref[0].astype(BF16)
    ikb_ref[0, P:P + T, :] = in_ref[0]
    ikb_ref[0, P + T:Lp, :] = jnp.zeros((Lp - P - T, IDX_DIM), BF16)


def _assemble_cache(ck, cv, cki, k_new, vT_new, ik_new, Lp):
    B, P = ck.shape[:2]
    T = k_new.shape[1]
    lead = lambda *blk: pl.BlockSpec((1,) + blk, lambda b: (b,) + (0,) * len(blk))
    return pl.pallas_call(
        _cache_kernel,
        grid=(B,),
        in_specs=[lead(P * N_HEADS, HEAD_DIM), lead(P * N_HEADS, HEAD_DIM), lead(P, IDX_DIM),
                  lead(T, WIDTH), lead(N_HEADS, VT_ROWS, T), lead(T, IDX_DIM)],
        out_specs=[lead(Lp, WIDTH), lead(N_HEADS, VT_ROWS, Lp), lead(Lp, IDX_DIM)],
        out_shape=[jax.ShapeDtypeStruct((B, Lp, WIDTH), BF16),
                   jax.ShapeDtypeStruct((B, N_HEADS, VT_ROWS, Lp), BF16),
                   jax.ShapeDtypeStruct((B, Lp, IDX_DIM), BF16)],
        compiler_params=_params(("parallel",)),
        name="assemble_cache",
    )(ck.reshape(B, P * N_HEADS, HEAD_DIM), cv.reshape(B, P * N_HEADS, HEAD_DIM), cki, k_new, vT_new, ik_new)


def _key_limit(q_first, q_count, P, Lreal):
    return jnp.minimum(Lreal, ((P + q_first + q_count - 1) // CHUNK + 1) * CHUNK)


def _sorting_program(n, keep):
    pairs = []

    def merge(lo, m, r):
        step = r * 2
        if step < m:
            merge(lo, m, step)
            merge(lo + r, m, step)
            pairs.extend((i, i + r) for i in range(lo + r, lo + m - r, step))
        else:
            pairs.append((lo, lo + r))

    def sort(lo, m):
        if m > 1:
            sort(lo, m // 2)
            sort(lo + m // 2, m // 2)
            merge(lo, m, 1)

    sort(0, n)
    need, prog = set(range(keep)), []
    for i, j in reversed(pairs):
        if i in need or j in need:
            prog.append((i, j, i in need, j in need))
            need.update((i, j))
    return prog[::-1]


SORT_GROUP = 16
SORT_LEVELS = 8
SORT_MIN_KEYS = 6400
SORT_PROGRAM = _sorting_program(SORT_GROUP, SORT_LEVELS)


def _idx_kernel(iq_ref, ik_ref, iwT_ref, keys_ref, tn_ref, lev_scr, *, TQ, TK, P, Lreal, topk):
    qb = pl.program_id(1)
    nkt = (_key_limit(qb * TQ, TQ, P, Lreal) + TK - 1) // TK
    tq = P + qb * TQ + lax.broadcasted_iota(I32, (1, TQ), 1)
    visible = jnp.minimum(Lreal, ((tq >> 6) + 1) << 6)
    lrow = lax.broadcasted_iota(I32, (TK, TQ), 0)
    LR = TK // SORT_GROUP
    use_levels = nkt * TK >= SORT_MIN_KEYS

    def tile_body(kt, carry):
        r0 = pl.multiple_of(kt * TK, TK)
        ik = ik_ref[0, pl.ds(r0, TK), :]
        acc = jnp.zeros((TK, TQ), F32)
        for h in range(IDX_HEADS):
            qh = iq_ref[0, :, h * IDX_DIM:(h + 1) * IDX_DIM]
            r = lax.dot_general(ik, qh, (((1,), (1,)), ((), ())), preferred_element_type=F32)
            acc = acc + jnp.maximum(r, 0.0) * iwT_ref[0, 0, h:h + 1, :]
        bits = pltpu.bitcast(acc, I32)
        key = bits ^ ((bits >> 31) & 0x7FFFFFFF)
        key = jnp.where(lrow < visible - r0, key, INT_MIN)
        keys_ref[0, 0, pl.ds(r0, TK), :] = key

        x = key.reshape(TK // (SORT_GROUP * 8), SORT_GROUP, 8, TQ)
        w = [x[:, j] for j in range(SORT_GROUP)]
        for i, j, need_max, need_min in SORT_PROGRAM:
            hi, lo = jnp.maximum(w[i], w[j]), jnp.minimum(w[i], w[j])
            if need_max:
                w[i] = hi
            if need_min:
                w[j] = lo
        for v in range(SORT_LEVELS):
            lev_scr[v, pl.ds(pl.multiple_of(kt * LR, LR), LR), :] = w[v].reshape(LR, TQ)
        return carry

    lax.fori_loop(0, nkt, tile_body, 0)

    def fill_body(kt, carry):
        keys_ref[0, 0, pl.ds(pl.multiple_of(kt * TK, TK), TK), :] = jnp.full((TK, TQ), INT_MIN, I32)
        return carry

    lax.fori_loop(nkt, keys_ref.shape[2] // TK, fill_body, 0)

    SUB = 64

    def count(pred):
        def body(kt, acc):
            r0 = pl.multiple_of(kt * TK, TK)
            for s in range(TK // SUB):
                blk = keys_ref[0, 0, pl.ds(r0 + s * SUB, SUB), :]
                acc = acc + pred(blk).astype(I32)
            return acc
        acc = lax.fori_loop(0, nkt, body, jnp.zeros((SUB, TQ), I32))
        return jnp.sum(acc, axis=0, keepdims=True)

    def count_levels(pred, levels):
        def body(kt, acc):
            r0 = pl.multiple_of(kt * LR, LR)
            for v in levels:
                acc = acc + pred(lev_scr[v, pl.ds(r0, LR), :]).astype(I32)
            return acc
        acc = lax.fori_loop(0, nkt, body, jnp.zeros((LR, TQ), I32))
        return jnp.sum(acc, axis=0, keepdims=True)

    def search(count_fn):
        def pass_body(b, carry):
            t_u, cnt_t = carry
            cand_u = t_u | lax.shift_left(jnp.int32(1), jnp.asarray(31 - b, I32))
            cand_s = cand_u ^ INT_MIN
            cnt = count_fn(lambda blk: blk >= cand_s)
            take = cnt >= topk
            return jnp.where(take, cand_u, t_u), jnp.where(take, cnt, cnt_t)
        return lax.fori_loop(0, 32, pass_body,
                             (jnp.zeros((1, TQ), I32), jnp.full((1, TQ), TAKE_ALL, I32)))

    scanned = range(SORT_LEVELS - 1)

    def search_levels():
        t_u, cnt_t = search(lambda pred: count_levels(pred, scanned))
        t_s = t_u ^ INT_MIN
        missed = count_levels(lambda blk: blk >= t_s, [SORT_LEVELS - 1])
        return t_u, cnt_t, jnp.max(missed)

    t_u, cnt_t, missed = lax.cond(
        use_levels, search_levels,
        lambda: (jnp.zeros((1, TQ), I32), jnp.full((1, TQ), TAKE_ALL, I32), jnp.int32(1)))
    full_scan = missed > 0
    t_u, cnt_t = lax.cond(full_scan, lambda: search(count), lambda: (t_u, cnt_t))
    t_s = t_u ^ INT_MIN
    ties = (cnt_t > topk) & (t_s != INT_MIN)
    any_ties = jnp.max(ties.astype(I32)) > 0
    count_gt = lambda: lax.cond(full_scan, lambda: count(lambda blk: blk > t_s),
                                lambda: count_levels(lambda blk: blk > t_s, scanned))
    cnt_gt = lax.cond(any_ties, count_gt, lambda: jnp.zeros((1, TQ), I32))
    n_take = jnp.where(ties, topk - cnt_gt, TAKE_ALL)
    row = lax.broadcasted_iota(I32, (8, TQ), 0)
    tn_ref[0, 0] = jnp.where(row == 0, t_s, n_take)


def _index_select(iq, ikb, iwT, *, TQ, TK, P, Lreal, topk):
    nB, Lp, _ = ikb.shape
    nQ = iwT.shape[1]
    return pl.pallas_call(
        functools.partial(_idx_kernel, TQ=TQ, TK=TK, P=P, Lreal=Lreal, topk=topk),
        grid=(nB, nQ),
        in_specs=[pl.BlockSpec((1, TQ, IDX_HEADS * IDX_DIM), lambda b, q: (b, q, 0)),
                  pl.BlockSpec((1, Lp, IDX_DIM), lambda b, q: (b, 0, 0)),
                  pl.BlockSpec((1, 1, IDX_HEADS, TQ), lambda b, q: (b, q, 0, 0))],
        out_specs=[pl.BlockSpec((1, 1, Lp, TQ), lambda b, q: (b, q, 0, 0)),
                   pl.BlockSpec((1, 1, 8, TQ), lambda b, q: (b, q, 0, 0))],
        out_shape=[jax.ShapeDtypeStruct((nB, nQ, Lp, TQ), I32),
                   jax.ShapeDtypeStruct((nB, nQ, 8, TQ), I32)],
        scratch_shapes=[pltpu.VMEM((SORT_LEVELS, Lp // SORT_GROUP, TQ), I32)],
        compiler_params=_params(("parallel", "arbitrary")),
        name="index_select",
    )(iq, ikb, iwT)


def _attn_kernel(qbs_ref, kts_ref, last_ref, q_ref, k_ref, vT_ref, keys_ref, tn_ref, o_ref,
                 qT_scr, ndm_scr, a_scr, m_scr, l_scr, acc_scr, tie_scr, *, TQ, TK, KC, P, Lreal):
    s_id = pl.program_id(1)
    qb = qbs_ref[s_id]
    kt = kts_ref[s_id]
    is_last = last_ref[s_id] == 1
    heads = [slice(h * HEAD_DIM, (h + 1) * HEAD_DIM) for h in range(N_HEADS)]

    @pl.when(kt == 0)
    def _():
        m_scr[...] = jnp.full(m_scr.shape, -jnp.inf, F32)
        l_scr[...] = jnp.zeros(l_scr.shape, F32)
        acc_scr[...] = jnp.zeros(acc_scr.shape, F32)
        tie_scr[...] = jnp.zeros(tie_scr.shape, F32)
        for sl in heads:
            qT_scr[sl, :] = q_ref[0, :, sl].astype(F32).T.astype(BF16)

    keys = keys_ref[0, 0]
    thr = tn_ref[0, 0, 0:1, :]
    ntk = tn_ref[0, 0, 1:2, :]
    l = kt * TK + lax.broadcasted_iota(I32, (TK, TQ), 0)
    tq = P + qb * TQ + lax.broadcasted_iota(I32, (TK, TQ), 1)
    has_ties = jnp.max(jnp.where(ntk == TAKE_ALL, 0, 1)) > 0
    no_ties = jnp.logical_not(has_ties)

    @pl.when(jnp.logical_and(no_ties, jnp.logical_not(is_last)))
    def _():
        ndm_scr[...] = jnp.where(keys >= thr, (l - tq).astype(F32), -MASKED_DIST)

    @pl.when(jnp.logical_and(no_ties, is_last))
    def _():
        allowed = (l < Lreal) & ((l >> 6) <= (tq >> 6))
        ndm_scr[...] = jnp.where((keys >= thr) & allowed, -jnp.abs(tq - l).astype(F32), -MASKED_DIST)

    @pl.when(has_ties)
    def _():
        allowed = (l < Lreal) & ((l >> 6) <= (tq >> 6))
        eq = keys == thr
        li = lax.broadcasted_iota(I32, (TK, TK), 0)
        lj = lax.broadcasted_iota(I32, (TK, TK), 1)
        lower = jnp.where(lj <= li, 1.0, 0.0).astype(BF16)
        prefix = jnp.dot(lower, jnp.where(eq, 1.0, 0.0).astype(BF16), preferred_element_type=F32)
        rank = tie_scr[0:1, :] + prefix
        sel = ((keys > thr) | (eq & (rank <= ntk.astype(F32)))) & allowed
        ndm_scr[...] = jnp.where(sel, -jnp.abs(tq - l).astype(F32), -MASKED_DIST)
        tie_scr[0:1, :] = tie_scr[0:1, :] + prefix[TK - 1:TK, :]

    chunks = [slice(c * KC, (c + 1) * KC) for c in range(TK // KC)]

    def stage_a(h, rows, mx):
        s = jnp.dot(k_ref[0, rows, heads[h]], qT_scr[heads[h], :], preferred_element_type=F32)
        a = s + (ALIBI_SLOPES[h] * LOG2E) * ndm_scr[rows, :]
        a_scr[h % n_slots, rows, :] = a
        cm = jnp.max(a, axis=0, keepdims=True)
        return cm if mx is None else jnp.maximum(mx, cm)

    def stage_b(h, rows, m_new, pv):
        p = jnp.exp2(a_scr[h % n_slots, rows, :] - m_new).astype(BF16)
        d = jnp.dot(vT_ref[0, h, :, rows], p, preferred_element_type=F32)
        return d if pv is None else pv + d

    m_all = m_scr[...]
    l_all = l_scr[...]
    m_out, l_out = [], []
    n_slots = a_scr.shape[0]
    ahead = n_slots - 1
    mxs = [None] * (N_HEADS + ahead)
    for h0 in range(ahead):
        for rows in chunks:
            mxs[h0] = stage_a(h0, rows, mxs[h0])
    for h in range(N_HEADS):
        m_prev = m_all[h:h + 1]
        m_new = jnp.maximum(m_prev, mxs[h])
        alpha = jnp.exp2(m_prev - m_new)
        pv = None
        for rows in chunks:
            if h + ahead < N_HEADS:
                mxs[h + ahead] = stage_a(h + ahead, rows, mxs[h + ahead])
            pv = stage_b(h, rows, m_new, pv)
        acc_scr[heads[h], :] = alpha * acc_scr[heads[h], :] + pv[0:HEAD_DIM]
        l_out.append(alpha * l_all[h:h + 1] + pv[HEAD_DIM:HEAD_DIM + 1])
        m_out.append(m_new)
    m_scr[...] = jnp.concatenate(m_out, axis=0)
    l_scr[...] = jnp.concatenate(l_out, axis=0)

    @pl.when(is_last)
    def _():
        for h in range(N_HEADS):
            o_ref[0, :, heads[h]] = (acc_scr[heads[h], :] / l_scr[h:h + 1, :]).T.astype(BF16)


def _attention(aq, kb, vT, keysT, tn, *, TQ, TK, KC, P, Lreal):
    nB, Tq, _ = aq.shape
    nQ = Tq // TQ
    qbs, kts, last = [], [], []
    for qb in range(nQ):
        limit = min(Lreal, ((P + qb * TQ + TQ - 1) // CHUNK + 1) * CHUNK)
        n = -(-limit // TK)
        assert (n - 1) * TK <= P + qb * TQ
        qbs += [qb] * n
        kts += list(range(n))
        last += [0] * (n - 1) + [1]
    steps = len(qbs)
    qmap = lambda b, s, qbs, kts, last: (b, qbs[s], 0)
    grid_spec = pltpu.PrefetchScalarGridSpec(
        num_scalar_prefetch=3,
        grid=(nB, steps),
        in_specs=[pl.BlockSpec((1, TQ, WIDTH), qmap),
                  pl.BlockSpec((1, TK, WIDTH), lambda b, s, qbs, kts, last: (b, kts[s], 0)),
                  pl.BlockSpec((1, N_HEADS, VT_ROWS, TK), lambda b, s, qbs, kts, last: (b, 0, 0, kts[s])),
                  pl.BlockSpec((1, 1, TK, TQ), lambda b, s, qbs, kts, last: (b, qbs[s], kts[s], 0)),
                  pl.BlockSpec((1, 1, 8, TQ), lambda b, s, qbs, kts, last: (b, qbs[s], 0, 0))],
        out_specs=pl.BlockSpec((1, TQ, WIDTH), qmap),
        scratch_shapes=[pltpu.VMEM((WIDTH, TQ), BF16),
                        pltpu.VMEM((TK, TQ), F32),
                        pltpu.VMEM((ATTN_SCORE_SLOTS, TK, TQ), F32),
                        pltpu.VMEM((N_HEADS, TQ), F32),
                        pltpu.VMEM((N_HEADS, TQ), F32),
                        pltpu.VMEM((WIDTH, TQ), F32),
                        pltpu.VMEM((8, TQ), F32)])
    arr = lambda v: jnp.asarray(np.array(v, np.int32))
    return pl.pallas_call(
        functools.partial(_attn_kernel, TQ=TQ, TK=TK, KC=KC, P=P, Lreal=Lreal),
        grid_spec=grid_spec,
        out_shape=jax.ShapeDtypeStruct((nB, Tq, WIDTH), BF16),
        compiler_params=_params(("parallel", "arbitrary")),
        name="sparse_attention",
    )(arr(qbs), arr(kts), arr(last), aq, kb, vT, keysT, tn)


def _outproj_kernel(attn_ref, ret_ref, x_ref, gate_ref, shift_ref, scale_ref, g2_ref, wo_ref, wrh_ref, wrl_ref,
                    x1_ref, h2_ref, comb_ref, *, bb, tt):
    D = x_ref.shape[-1]
    tm = bb * tt
    CW = 512
    chunks = [slice(c * CW, (c + 1) * CW) for c in range(D // CW)]

    ssq = jnp.zeros((tm, 1), F32)
    for cols in chunks:
        mix = (jnp.dot(attn_ref[...], wo_ref[0:WIDTH, cols], preferred_element_type=F32)
               + jnp.dot(ret_ref[...], wo_ref[WIDTH:2 * WIDTH, cols], preferred_element_type=F32))
        x1 = x_ref[:, cols].reshape(bb, tt, CW) + gate_ref[:, :, cols] * mix.reshape(bb, tt, CW)
        x1 = x1.reshape(tm, CW)
        x1_ref[:, cols] = x1
        ssq = ssq + jnp.sum(x1 * x1, axis=-1, keepdims=True)
    rinv = lax.rsqrt(ssq * (1.0 / D) + EPS)

    logits = jnp.zeros((tm, LANES), F32)
    for cols in chunks:
        h2 = (x1_ref[:, cols] * rinv * g2_ref[:, cols]).reshape(bb, tt, CW)
        h2 = (h2 * (1.0 + scale_ref[:, :, cols]) + shift_ref[:, :, cols]).reshape(tm, CW)
        h_hi = h2.astype(BF16)
        h2_ref[:, cols] = h_hi
        h_lo = (h2 - h_hi.astype(F32)).astype(BF16)
        logits = (logits + jnp.dot(h_hi, wrh_ref[cols, :], preferred_element_type=F32)
                  + jnp.dot(h_lo, wrh_ref[cols, :], preferred_element_type=F32)
                  + jnp.dot(h_hi, wrl_ref[cols, :], preferred_element_type=F32))
    lane = lax.broadcasted_iota(I32, logits.shape, 1)
    ninf = -jnp.inf
    rmax = lambda v: jnp.max(v, axis=-1, keepdims=True)
    rsum = lambda v: jnp.sum(v, axis=-1, keepdims=True)
    first = lambda m: jnp.min(jnp.where(m, lane, LANES), axis=-1, keepdims=True)
    gl = jnp.where(lane < N_GROUPS, logits, ninf)
    gmax = rmax(gl)
    g_top = 1.0 / rsum(jnp.exp(gl - gmax))
    g_idx = first(gl == gmax)
    emask = (lane >= N_GROUPS) & (lane < N_GROUPS + N_EXPERTS) & (((lane - N_GROUPS) >> 2) == g_idx)
    el = jnp.where(emask, logits, ninf)
    emax = rmax(el)
    esum = rsum(jnp.exp(el - emax))
    i1 = first(el == emax)
    el2 = jnp.where(lane == i1, ninf, el)
    emax2 = rmax(el2)
    i2 = first(el2 == emax2)
    p1 = 1.0 / esum
    p2 = jnp.exp(emax2 - emax) / esum
    den = p1 + p2
    comb_ref[...] = jnp.where(lane == i1, g_top * (p1 / den),
                              jnp.where(lane == i2, g_top * (p2 / den), 0.0))


def _outproj(attn, ret, x2d, gate1, shift2, scale2, g2, wo, wr_hi, wr_lo, *, bb, tt):
    N, D = x2d.shape
    tm = bb * tt
    tok = lambda w: pl.BlockSpec((tm, w), lambda i: (i, 0))
    mod = pl.BlockSpec((bb, 1, D), lambda i: (i if bb > 1 else 0, 0, 0))
    full = lambda a: pl.BlockSpec(a.shape, lambda i: (0,) * a.ndim)
    return pl.pallas_call(
        functools.partial(_outproj_kernel, bb=bb, tt=tt),
        grid=(N // tm,),
        in_specs=[tok(WIDTH), tok(WIDTH), tok(D), mod, mod, mod, full(g2), full(wo), full(wr_hi), full(wr_lo)],
        out_specs=[tok(D), tok(D), tok(LANES)],
        out_shape=[jax.ShapeDtypeStruct((N, D), F32), jax.ShapeDtypeStruct((N, D), BF16),
                   jax.ShapeDtypeStruct((N, LANES), F32)],
        compiler_params=_params(("parallel",)),
        name="outproj_router",
    )(attn, ret, x2d, gate1, shift2, scale2, g2, wo, wr_hi, wr_lo)


def _moe_kernel(h_ref, comb_ref, x1_ref, gate_ref, wg_ref, wu_ref, wd_ref, y_ref, acc_scr, *, bb, tt):
    e = pl.program_id(1)
    D = x1_ref.shape[-1]

    @pl.when(e == 0)
    def _():
        acc_scr[...] = jnp.zeros(acc_scr.shape, F32)

    h = h_ref[...]
    a = jnp.dot(h, wg_ref[0], preferred_element_type=F32)
    b = jnp.dot(h, wu_ref[0], preferred_element_type=F32)
    mid = (_silu(a) * b).astype(BF16)
    out = jnp.dot(mid, wd_ref[0], preferred_element_type=F32)
    lane = lax.broadcasted_iota(I32, comb_ref.shape, 1)
    w = jnp.sum(jnp.where(lane == e + N_GROUPS, comb_ref[...], 0.0), axis=-1, keepdims=True)
    acc_scr[...] += w * out

    @pl.when(e == N_EXPERTS - 1)
    def _():
        y = x1_ref[...].reshape(bb, tt, D) + gate_ref[...] * acc_scr[...].reshape(bb, tt, D)
        y_ref[...] = y.reshape(bb * tt, D)


def _moe(h2, comb, x1, gate2, wg, wu, wd, *, bb, tt):
    N, D = x1.shape
    tm = bb * tt
    tok = lambda w: pl.BlockSpec((tm, w), lambda i, e: (i, 0))
    mod = pl.BlockSpec((bb, 1, D), lambda i, e: (i if bb > 1 else 0, 0, 0))
    return pl.pallas_call(
        functools.partial(_moe_kernel, bb=bb, tt=tt),
        grid=(N // tm, N_EXPERTS),
        in_specs=[tok(D), tok(LANES), tok(D), mod,
                  pl.BlockSpec((1, D, D_EXPERT), lambda i, e: (e, 0, 0)),
                  pl.BlockSpec((1, D, D_EXPERT), lambda i, e: (e, 0, 0)),
                  pl.BlockSpec((1, D_EXPERT, D), lambda i, e: (e, 0, 0))],
        out_specs=tok(D),
        out_shape=jax.ShapeDtypeStruct((N, D), F32),
        scratch_shapes=[pltpu.VMEM((tm, D), F32)],
        compiler_params=_params(("parallel", "arbitrary")),
        name="moe",
    )(h2, comb, x1, gate2, wg, wu, wd)


def _moe_routed_kernel(h_ref, comb_ref, x1_hbm, gate_ref, wg_ref, wu_ref, wd_ref, y_ref,
                       rank_scr, rankT_scr, gT_scr, sem):
    i = pl.program_id(0)
    e = pl.program_id(1)
    tm, D = y_ref.shape
    R, KP = MOE_ROWS, MOE_SCATTER_K

    @pl.when(e == 0)
    def _():
        residual = pltpu.make_async_copy(x1_hbm.at[pl.ds(i * tm, tm), :], y_ref, sem)
        residual.start()
        comb = comb_ref[...]
        used = comb != 0.0
        ti = lax.broadcasted_iota(I32, (tm, tm), 0)
        tj = lax.broadcasted_iota(I32, (tm, tm), 1)
        earlier = jnp.where(tj < ti, 1.0, 0.0).astype(BF16)
        rank = jnp.dot(earlier, jnp.where(used, 1.0, 0.0).astype(BF16), preferred_element_type=F32)
        rank = jnp.where(used, rank, -1.0)
        rank_scr[...] = rank
        rankT_scr[...] = rank.T
        gT_scr[...] = comb.T
        residual.wait()

    lane_e = e + N_GROUPS
    lane = lax.broadcasted_iota(I32, (tm, LANES), 1)
    r_col = jnp.sum(jnp.where(lane == lane_e, rank_scr[...], 0.0), axis=-1, keepdims=True)
    r_row = rankT_scr[pl.ds(lane_e, 1), :]
    g_row = gT_scr[pl.ds(lane_e, 1), :]
    n_e = jnp.sum(jnp.where(r_row >= 0.0, 1, 0))
    gate2 = gate_ref[0]
    CW = 512

    def chunk(c, carry):
        base = jnp.asarray(c * R, F32)
        ridx = base + lax.broadcasted_iota(I32, (R, tm), 0).astype(F32)
        pm = r_row == ridx
        x = jnp.dot(jnp.where(pm, 1.0, 0.0).astype(BF16), h_ref[...], preferred_element_type=F32).astype(BF16)
        a = jnp.dot(x, wg_ref[0], preferred_element_type=F32)
        b = jnp.dot(x, wu_ref[0], preferred_element_type=F32)
        mid = (_silu(a) * b).astype(BF16)
        y = jnp.dot(mid, wd_ref[0], preferred_element_type=F32)
        g_r = jnp.sum(jnp.where(pm, g_row, 0.0), axis=-1, keepdims=True)
        ys = (y * g_r * gate2).astype(BF16)
        ys = jnp.concatenate([ys, jnp.zeros((KP - R, D), BF16)], axis=0)
        cidx = lax.broadcasted_iota(I32, (tm, KP), 1)
        sm = (r_col == base + cidx.astype(F32)) & (cidx < R)
        s = jnp.where(sm, 1.0, 0.0).astype(BF16)
        for c0 in range(0, D, CW):
            y_ref[:, c0:c0 + CW] += jnp.dot(s, ys[:, c0:c0 + CW], preferred_element_type=F32)
        return carry

    lax.fori_loop(0, (n_e + R - 1) // R, chunk, 0)


def _moe_routed(h2, comb, x1, gate2, wg, wu, wd):
    N, D = x1.shape
    tm = MOE_TOKENS
    assert gate2.shape[0] == 1 and N % tm == 0
    tok = lambda w: pl.BlockSpec((tm, w), lambda i, e: (i, 0))
    return pl.pallas_call(
        _moe_routed_kernel,
        grid=(N // tm, N_EXPERTS),
        in_specs=[tok(D), tok(LANES), pl.BlockSpec(memory_space=pl.ANY),
                  pl.BlockSpec((1, 1, D), lambda i, e: (0, 0, 0)),
                  pl.BlockSpec((1, D, D_EXPERT), lambda i, e: (e, 0, 0)),
                  pl.BlockSpec((1, D, D_EXPERT), lambda i, e: (e, 0, 0)),
                  pl.BlockSpec((1, D_EXPERT, D), lambda i, e: (e, 0, 0))],
        out_specs=tok(D),
        out_shape=jax.ShapeDtypeStruct((N, D), F32),
        scratch_shapes=[pltpu.VMEM((tm, LANES), F32), pltpu.VMEM((LANES, tm), F32),
                        pltpu.VMEM((LANES, tm), F32), pltpu.SemaphoreType.DMA(())],
        compiler_params=_params(("parallel", "arbitrary")),
        name="moe_routed",
    )(h2, comb, x1, gate2, wg, wu, wd)


def _layer(x, mod, past, W, *, bb, tt, TQ, TK, KC, ret_chunk):
    B, T, D = x.shape
    N = B * T
    x2d = x.reshape(N, D)
    shift1, scale1, gate1, shift2, scale2, gate2 = [m.reshape(B, 1, D) for m in jnp.split(mod, 6, axis=-1)]
    z = _inproj(x2d, shift1, scale1, W["g1"], W["w_in"], W["qg"], W["kg"], bb=bb, tt=tt)
    vT_new = z["avT"].reshape(N_HEADS, VT_ROWS, B, T)
    vT_new = vT_new.reshape(1, N_HEADS, VT_ROWS, T) if B == 1 else vT_new.transpose(2, 0, 1, 3)

    if past is None:
        state0 = jnp.zeros((B, N_HEADS, HEAD_DIM, HEAD_DIM), F32)
        P, Lreal = 0, T
        topk = min(TOPK_MAX, T // 4)
        kb = z["akb"].reshape(B, T, WIDTH)
        vT = vT_new
        ikb = z["ikb"].reshape(B, T, IDX_DIM)
        Tq = T
        pad_q = lambda a: a.reshape(B, T, a.shape[-1])
    else:
        ck, cv, cki, state0 = past
        P = ck.shape[1]
        Lreal = P + T
        topk = min(TOPK_MAX, Lreal // 4)
        Lp = -(-Lreal // TK) * TK
        kb, vT, ikb = _assemble_cache(ck, cv, cki, z["akb"].reshape(B, T, WIDTH), vT_new,
                                      z["ikb"].reshape(B, T, IDX_DIM), Lp)
        Tq = -(-T // TQ) * TQ
        pad_q = lambda a: jnp.pad(a.reshape(B, T, a.shape[-1]), ((0, 0), (0, Tq - T), (0, 0)))

    iq = pad_q(z["iq"])
    iwT = pad_q(z["iw"]).reshape(B, Tq // TQ, TQ, IDX_HEADS).transpose(0, 1, 3, 2)
    keysT, tn = _index_select(iq, ikb, iwT, TQ=TQ, TK=TK, P=P, Lreal=Lreal, topk=topk)
    attn = _attention(pad_q(z["aq"]), kb, vT, keysT, tn, TQ=TQ, TK=TK, KC=KC, P=P, Lreal=Lreal)
    attn = attn[:, :T].reshape(N, WIDTH)

    ret, ret_state = _retention(z["rq"], z["rk"], z["rv"], z["rg"], state0, B=B, T=T, C=ret_chunk)
    x1, h2, comb = _outproj(attn, ret, x2d, gate1, shift2, scale2, W["g2"], W["w_out"], W["w_r_hi"],
                            W["w_r_lo"], bb=bb, tt=tt)
    if B == 1 and N % MOE_TOKENS == 0:
        y = _moe_routed(h2, comb, x1, gate2, W["wg"], W["wu"], W["wd"])
    else:
        y = _moe(h2, comb, x1, gate2, W["wg"], W["wu"], W["wd"], bb=bb, tt=tt)
    return (y.reshape(B, T, D), z["ak"].reshape(B, T, N_HEADS, HEAD_DIM),
            z["av"].reshape(B, T, N_HEADS, HEAD_DIM), z["ik"].reshape(B, T, IDX_DIM), ret_state)


def kernel(x_prompt, x_sample, cache_k, cache_v, cache_kidx, state_ret, c_prompt, c_sample, w_ada, b_ada,
           norm1_g, norm2_g, w_in, q_norm_g, k_norm_g, w_out, w_group, w_router, w_gate_e, w_up_e, w_down_e):
    depth = w_ada.shape[0]
    D = D_MODEL
    Bp, Bs = x_prompt.shape[0], x_sample.shape[0]
    Ts = x_sample.shape[1]
    xp, xs = x_prompt, x_sample
    outs_p, outs_s = [], []
    for l in range(depth):
        w_r = jnp.concatenate(
            [w_group[l], jnp.moveaxis(w_router[l], 0, 1).reshape(D, N_EXPERTS),
             jnp.zeros((D, LANES - N_GROUPS - N_EXPERTS), F32)], axis=1)
        w_r_hi = w_r.astype(BF16)
        W = {
            "g1": norm1_g[l].reshape(1, D), "g2": norm2_g[l].reshape(1, D),
            "qg": q_norm_g[l].reshape(1, HEAD_DIM), "kg": k_norm_g[l].reshape(1, HEAD_DIM),
            "w_in": _pad_cast_w_in(w_in[l]),
            "w_out": w_out[l].astype(BF16),
            "w_r_hi": w_r_hi, "w_r_lo": (w_r - w_r_hi.astype(F32)).astype(BF16),
            "wg": w_gate_e[l].astype(BF16), "wu": w_up_e[l].astype(BF16), "wd": w_down_e[l].astype(BF16),
        }
        rows = Bp + Bs
        rows_p = -(-rows // 8) * 8
        c_all = jnp.concatenate([c_prompt, c_sample, jnp.zeros((rows_p - rows, D), F32)], axis=0)
        mod = _adaln_mod(c_all, w_ada[l], b_ada[l])
        past = (cache_k[l], cache_v[l], cache_kidx[l], state_ret[l])
        xp, kp, vp, kip, sp = _layer(xp, mod[:Bp], None, W, bb=1, tt=512, TQ=256, TK=512, KC=256,
                                     ret_chunk=256)
        xs, kn, vn, kin, sn = _layer(xs, mod[Bp:rows], past, W, bb=512 // Ts, tt=Ts, TQ=128, TK=384, KC=384,
                                     ret_chunk=Ts)
        outs_p.append((kp, vp, kip, sp))
        outs_s.append((kn, vn, kin, sn))
    st = lambda xs_, i: jnp.stack([o[i] for o in xs_])
    return (xp, xs, st(outs_p, 0), st(outs_p, 1), st(outs_p, 2), st(outs_p, 3),
            st(outs_s, 0), st(outs_s, 1), st(outs_s, 2), st(outs_s, 3))
```

```python
import functools
import math

import numpy as np
import jax
import jax.numpy as jnp
from jax import lax
from jax.experimental import pallas as pl
from jax.experimental.pallas import tpu as pltpu

F32 = jnp.float32
BF16 = jnp.bfloat16
I32 = jnp.int32

D_MODEL = 2048
CHUNK = 64
N_HEADS = 8
HEAD_DIM = 128
WIDTH = N_HEADS * HEAD_DIM
IDX_HEADS = 8
IDX_DIM = 64
TOPK_MAX = 256
ATTN_SCALE = HEAD_DIM ** -0.5
IDX_SCALE = IDX_DIM ** -0.5
IDX_W_SCALE = IDX_HEADS ** -0.5
RET_K_SCALE = HEAD_DIM ** -0.5
N_GROUPS = 4
EXPERTS_PER_GROUP = 4
N_EXPERTS = N_GROUPS * EXPERTS_PER_GROUP
D_EXPERT = 512
EPS = 1e-6

OFF_IQ = 3 * WIDTH
IDX_COLS = IDX_HEADS * IDX_DIM + IDX_DIM + IDX_HEADS
OFF_RQ = OFF_IQ + IDX_COLS
N_COL_GROUPS = 8

LANES = 128
INT_MIN = -(2 ** 31)
TAKE_ALL = 2 ** 30
MASKED_DIST = 1e33
VMEM_LIMIT = 56 * 1024 * 1024
ATTN_SCORE_SLOTS = 3
MOE_TOKENS = 1024
MOE_ROWS = 160
MOE_SCATTER_K = 256

LOG2E = 1.4426950408889634
VT_ROWS = HEAD_DIM + 16
ALIBI_SLOPES = [float(2.0 ** (-8.0 * (h + 1) / N_HEADS)) for h in range(N_HEADS)]
LOG_GAMMA = [float(np.log1p(-(2.0 ** (-5.0 - h)))) for h in range(N_HEADS)]


def _params(sem):
    return pltpu.CompilerParams(dimension_semantics=sem, vmem_limit_bytes=VMEM_LIMIT)


def _silu(x):
    return x * jax.nn.sigmoid(x)


def _mod_kernel(c_ref, w_ref, b_ref, o_ref):
    s = _silu(c_ref[...])
    o_ref[...] = jnp.dot(s, w_ref[...], precision=lax.Precision.HIGHEST,
                         preferred_element_type=F32) + b_ref[...]


def _adaln_mod(c, w_ada, b_ada):
    R, D = c.shape
    N = w_ada.shape[1]
    tn = 1024
    return pl.pallas_call(
        _mod_kernel,
        grid=(N // tn,),
        in_specs=[pl.BlockSpec((R, D), lambda j: (0, 0)),
                  pl.BlockSpec((D, tn), lambda j: (0, j)),
                  pl.BlockSpec((1, tn), lambda j: (0, j))],
        out_specs=pl.BlockSpec((R, tn), lambda j: (0, j)),
        out_shape=jax.ShapeDtypeStruct((R, N), F32),
        compiler_params=_params(("arbitrary",)),
        name="adaln_mod",
    )(c, w_ada, b_ada.reshape(1, N))


def _w_in_kernel(w_ref, o_ref):
    rows = w_ref.shape[0]
    pad = o_ref.shape[1] - w_ref.shape[1]
    o_ref[:, 0:OFF_RQ] = w_ref[:, 0:OFF_RQ].astype(BF16)
    o_ref[:, OFF_RQ:OFF_RQ + pad] = jnp.zeros((rows, pad), BF16)
    o_ref[:, OFF_RQ + pad:] = w_ref[:, OFF_RQ:].astype(BF16)


def _pad_cast_w_in(w):
    D, n_in = w.shape
    n_out = N_COL_GROUPS * WIDTH
    rows = 128
    return pl.pallas_call(
        _w_in_kernel,
        grid=(D // rows,),
        in_specs=[pl.BlockSpec((rows, n_in), lambda i: (i, 0))],
        out_specs=pl.BlockSpec((rows, n_out), lambda i: (i, 0)),
        out_shape=jax.ShapeDtypeStruct((D, n_out), BF16),
        compiler_params=_params(("parallel",)),
        name="pad_cast_w_in",
    )(w)


def _inproj_kernel(x_ref, shift_ref, scale_ref, g1_ref, w_ref, qg_ref, kg_ref,
                   aq_ref, ak_ref, akb_ref, av_ref, avT_ref, iq_ref, ik_ref, ikb_ref, iw_ref,
                   rq_ref, rk_ref, rv_ref, rg_ref, h_scr, *, bb, tt):
    j = pl.program_id(1)
    D = x_ref.shape[-1]

    @pl.when(j == 0)
    def _():
        x = x_ref[...]
        ms = jnp.mean(x * x, axis=-1, keepdims=True)
        y = x * lax.rsqrt(ms + EPS) * g1_ref[...]
        y = y.reshape(bb, tt, D) * (1.0 + scale_ref[...]) + shift_ref[...]
        h_scr[...] = y.reshape(bb * tt, D).astype(BF16)

    CW = 2 * HEAD_DIM
    chunks = [slice(c * CW, (c + 1) * CW) for c in range(WIDTH // CW)]

    def proj(cols):
        return jnp.dot(h_scr[...], w_ref[:, cols], preferred_element_type=F32)

    def head_rms(zc, g_ref):
        outs = []
        for hh in range(CW // HEAD_DIM):
            zh = zc[:, hh * HEAD_DIM:(hh + 1) * HEAD_DIM]
            ms = jnp.mean(zh * zh, axis=-1, keepdims=True)
            outs.append(zh * lax.rsqrt(ms + EPS) * g_ref[...])
        return jnp.concatenate(outs, axis=1)

    @pl.when(j == 0)
    def _():
        for cols in chunks:
            v = head_rms(proj(cols), qg_ref)
            aq_ref[:, cols] = (v * (ATTN_SCALE * LOG2E)).astype(BF16)

    def store_heads(ref, c, v):
        tm = v.shape[0]
        for hh in range(CW // HEAD_DIM):
            h = c * (CW // HEAD_DIM) + hh
            ref[pl.ds(h, tm, stride=N_HEADS), :] = v[:, hh * HEAD_DIM:(hh + 1) * HEAD_DIM]

    @pl.when(j == 1)
    def _():
        for c, cols in enumerate(chunks):
            v = head_rms(proj(cols), kg_ref)
            akb_ref[:, cols] = v.astype(BF16)
            store_heads(ak_ref, c, v)

    @pl.when(j == 2)
    def _():
        for c, cols in enumerate(chunks):
            zc = proj(cols)
            store_heads(av_ref, c, zc)
            zT = zc.T
            for hh in range(CW // HEAD_DIM):
                h = c * (CW // HEAD_DIM) + hh
                avT_ref[h, 0:HEAD_DIM, :] = zT[hh * HEAD_DIM:(hh + 1) * HEAD_DIM, :].astype(BF16)
                avT_ref[h, HEAD_DIM:VT_ROWS, :] = jnp.ones((VT_ROWS - HEAD_DIM, zT.shape[1]), BF16)

    @pl.when(j == 3)
    def _():
        nq = IDX_HEADS * IDX_DIM
        for cols in chunks[:nq // CW]:
            iq_ref[:, cols] = (proj(cols) * IDX_SCALE).astype(BF16)
        zc = proj(chunks[nq // CW])
        ik = zc[:, 0:IDX_DIM]
        ik_ref[...] = ik
        ikb_ref[...] = ik.astype(BF16)
        iw_ref[...] = zc[:, IDX_DIM:IDX_DIM + IDX_HEADS] * IDX_W_SCALE

    @pl.when(j == 4)
    def _():
        for cols in chunks:
            rq_ref[:, cols] = proj(cols).astype(BF16)

    @pl.when(j == 5)
    def _():
        for cols in chunks:
            rk_ref[:, cols] = (proj(cols) * RET_K_SCALE).astype(BF16)

    @pl.when(j == 6)
    def _():
        for cols in chunks:
            rv_ref[:, cols] = proj(cols).astype(BF16)

    @pl.when(j == 7)
    def _():
        for cols in chunks:
            rg_ref[:, cols] = proj(cols)


def _inproj(x2d, shift, scale, g1, w_p, qg, kg, *, bb, tt):
    N, D = x2d.shape
    tm = bb * tt
    nI = N // tm
    tok = lambda w: pl.BlockSpec((tm, w), lambda i, j: (i, 0))
    mod = pl.BlockSpec((bb, 1, D), lambda i, j: (i if bb > 1 else 0, 0, 0))
    row = lambda w: pl.BlockSpec((1, w), lambda i, j: (0, 0))
    outs = [("aq", WIDTH, BF16), ("ak", "heads", F32), ("akb", WIDTH, BF16), ("av", "heads", F32),
            ("avT", "vT", BF16), ("iq", IDX_HEADS * IDX_DIM, BF16), ("ik", IDX_DIM, F32),
            ("ikb", IDX_DIM, BF16), ("iw", IDX_HEADS, F32), ("rq", WIDTH, BF16), ("rk", WIDTH, BF16),
            ("rv", WIDTH, BF16), ("rg", WIDTH, F32)]
    special = {"heads": (pl.BlockSpec((tm * N_HEADS, HEAD_DIM), lambda i, j: (i, 0)), (N * N_HEADS, HEAD_DIM)),
               "vT": (pl.BlockSpec((N_HEADS, VT_ROWS, tm), lambda i, j: (0, 0, i)), (N_HEADS, VT_ROWS, N))}
    out_specs = [special[w][0] if w in special else tok(w) for _, w, _ in outs]
    out_shape = [jax.ShapeDtypeStruct(special[w][1] if w in special else (N, w), dt) for _, w, dt in outs]
    res = pl.pallas_call(
        functools.partial(_inproj_kernel, bb=bb, tt=tt),
        grid=(nI, N_COL_GROUPS),
        in_specs=[tok(D), mod, mod, row(D),
                  pl.BlockSpec((D, WIDTH), lambda i, j: (0, j)), row(HEAD_DIM), row(HEAD_DIM)],
        out_specs=out_specs,
        out_shape=out_shape,
        scratch_shapes=[pltpu.VMEM((tm, D), BF16)],
        compiler_params=_params(("parallel", "arbitrary")),
        name="inproj",
    )(x2d, shift, scale, g1, w_p, qg, kg)
    return {name: r for (name, _, _), r in zip(outs, res)}


def _ret_kernel(q_ref, k_ref, v_ref, g_ref, s0_ref, o_ref, sn_ref, st_scr, *, C):
    c = pl.program_id(1)

    @pl.when(c == 0)
    def _():
        st_scr[...] = s0_ref[0]

    pi = lax.broadcasted_iota(I32, (C, C), 0)
    pj = lax.broadcasted_iota(I32, (C, C), 1)
    diff = (pi - pj).astype(F32)
    causal = pi >= pj
    pos = lax.broadcasted_iota(I32, (C, HEAD_DIM), 0).astype(F32)
    for h in range(N_HEADS):
        lg = LOG_GAMMA[h]
        sl = slice(h * HEAD_DIM, (h + 1) * HEAD_DIM)
        q = q_ref[:, sl]
        k = k_ref[:, sl]
        v = v_ref[:, sl]
        decay = jnp.where(causal, jnp.exp(lg * jnp.maximum(diff, 0.0)), 0.0)
        s = lax.dot_general(q, k, (((1,), (1,)), ((), ())), preferred_element_type=F32) * decay
        o = jnp.dot(s.astype(BF16), v, preferred_element_type=F32)
        st = st_scr[h]
        cross = jnp.exp(lg * (pos + 1.0))
        o = o + jnp.dot(q, st.astype(BF16), preferred_element_type=F32) * cross
        kdec = jnp.exp(lg * (C - 1.0 - pos))
        kd = (k.astype(F32) * kdec).astype(BF16)
        st_new = math.exp(lg * C) * st + lax.dot_general(
            kd, v, (((0,), (0,)), ((), ())), preferred_element_type=F32)
        st_scr[h] = st_new
        ms = jnp.mean(o * o, axis=-1, keepdims=True)
        o_ref[:, sl] = (o * lax.rsqrt(ms + EPS) * _silu(g_ref[:, sl])).astype(BF16)

    @pl.when(c == pl.num_programs(1) - 1)
    def _():
        sn_ref[0] = st_scr[...]


def _retention(rq, rk, rv, rg, state0, *, B, T, C):
    nC = T // C
    tok = pl.BlockSpec((C, WIDTH), lambda b, c: (b * nC + c, 0))
    st = pl.BlockSpec((1, N_HEADS, HEAD_DIM, HEAD_DIM), lambda b, c: (b, 0, 0, 0))
    return pl.pallas_call(
        functools.partial(_ret_kernel, C=C),
        grid=(B, nC),
        in_specs=[tok, tok, tok, tok, st],
        out_specs=[tok, st],
        out_shape=[jax.ShapeDtypeStruct((B * T, WIDTH), BF16),
                   jax.ShapeDtypeStruct((B, N_HEADS, HEAD_DIM, HEAD_DIM), F32)],
        scratch_shapes=[pltpu.VMEM((N_HEADS, HEAD_DIM, HEAD_DIM), F32)],
        compiler_params=_params(("parallel", "arbitrary")),
        name="retention",
    )(rq, rk, rv, rg, state0)


def _cache_kernel(ck_ref, cv_ref, cki_ref, kn_ref, vn_ref, in_ref, kb_ref, vT_ref, ikb_ref):
    P = cki_ref.shape[1]
    T = kn_ref.shape[1]
    Lp = kb_ref.shape[1]
    for h in range(N_HEADS):
        sl = slice(h * HEAD_DIM, (h + 1) * HEAD_DIM)
        kb_ref[0, 0:P, sl] = ck_ref[0, pl.ds(h, P, stride=N_HEADS), :].astype(BF16)
        vT_ref[0, h, 0:HEAD_DIM, 0:P] = cv_ref[0, pl.ds(h, P, stride=N_HEADS), :].T.astype(BF16)
        vT_ref[0, h, HEAD_DIM:VT_ROWS, 0:P] = jnp.ones((VT_ROWS - HEAD_DIM, P), BF16)
        vT_ref[0, h, :, P:Lp] = jnp.concatenate(
            [vn_ref[0, h], jnp.zeros((VT_ROWS, Lp - P - T), BF16)], axis=1)
    kb_ref[0, P:P + T, :] = kn_ref[0]
    kb_ref[0, P + T:Lp, :] = jnp.zeros((Lp - P - T, WIDTH), BF16)
    ikb_ref[0, 0:P, :] = cki_ref[0].astype(BF16)
    ikb_ref[0, P:P + T, :] = in_ref[0]
    ikb_ref[0, P + T:Lp, :] = jnp.zeros((Lp - P - T, IDX_DIM), BF16)


def _assemble_cache(ck, cv, cki, k_new, vT_new, ik_new, Lp):
    B, P = ck.shape[:2]
    T = k_new.shape[1]
    lead = lambda *blk: pl.BlockSpec((1,) + blk, lambda b: (b,) + (0,) * len(blk))
    return pl.pallas_call(
        _cache_kernel,
        grid=(B,),
        in_specs=[lead(P * N_HEADS, HEAD_DIM), lead(P * N_HEADS, HEAD_DIM), lead(P, IDX_DIM),
                  lead(T, WIDTH), lead(N_HEADS, VT_ROWS, T), lead(T, IDX_DIM)],
        out_specs=[lead(Lp, WIDTH), lead(N_HEADS, VT_ROWS, Lp), lead(Lp, IDX_DIM)],
        out_shape=[jax.ShapeDtypeStruct((B, Lp, WIDTH), BF16),
                   jax.ShapeDtypeStruct((B, N_HEADS, VT_ROWS, Lp), BF16),
                   jax.ShapeDtypeStruct((B, Lp, IDX_DIM), BF16)],
        compiler_params=_params(("parallel",)),
        name="assemble_cache",
    )(ck.reshape(B, P * N_HEADS, HEAD_DIM), cv.reshape(B, P * N_HEADS, HEAD_DIM), cki, k_new, vT_new, ik_new)


def _key_limit(q_first, q_count, P, Lreal):
    return jnp.minimum(Lreal, ((P + q_first + q_count - 1) // CHUNK + 1) * CHUNK)


def _sorting_program(n, keep):
    pairs = []

    def merge(lo, m, r):
        step = r * 2
        if step < m:
            merge(lo, m, step)
            merge(lo + r, m, step)
            pairs.extend((i, i + r) for i in range(lo + r, lo + m - r, step))
        else:
            pairs.append((lo, lo + r))

    def sort(lo, m):
        if m > 1:
            sort(lo, m // 2)
            sort(lo + m // 2, m // 2)
            merge(lo, m, 1)

    sort(0, n)
    need, prog = set(range(keep)), []
    for i, j in reversed(pairs):
        if i in need or j in need:
            prog.append((i, j, i in need, j in need))
            need.update((i, j))
    return prog[::-1]


SORT_GROUP = 16
SORT_LEVELS = 8
SORT_MIN_KEYS = 6400
SORT_PROGRAM = _sorting_program(SORT_GROUP, SORT_LEVELS)


def _idx_kernel(iq_ref, ik_ref, iwT_ref, keys_ref, tn_ref, lev_scr, *, TQ, TK, P, Lreal, topk):
    qb = pl.program_id(1)
    nkt = (_key_limit(qb * TQ, TQ, P, Lreal) + TK - 1) // TK
    tq = P + qb * TQ + lax.broadcasted_iota(I32, (1, TQ), 1)
    visible = jnp.minimum(Lreal, ((tq >> 6) + 1) << 6)
    lrow = lax.broadcasted_iota(I32, (TK, TQ), 0)
    LR = TK // SORT_GROUP
    use_levels = nkt * TK >= SORT_MIN_KEYS

    def tile_body(kt, carry):
        r0 = pl.multiple_of(kt * TK, TK)
        ik = ik_ref[0, pl.ds(r0, TK), :]
        acc = jnp.zeros((TK, TQ), F32)
        for h in range(IDX_HEADS):
            qh = iq_ref[0, :, h * IDX_DIM:(h + 1) * IDX_DIM]
            r = lax.dot_general(ik, qh, (((1,), (1,)), ((), ())), preferred_element_type=F32)
            acc = acc + jnp.maximum(r, 0.0) * iwT_ref[0, 0, h:h + 1, :]
        bits = pltpu.bitcast(acc, I32)
        key = bits ^ ((bits >> 31) & 0x7FFFFFFF)
        key = jnp.where(lrow < visible - r0, key, INT_MIN)
        keys_ref[0, 0, pl.ds(r0, TK), :] = key

        x = key.reshape(TK // (SORT_GROUP * 8), SORT_GROUP, 8, TQ)
        w = [x[:, j] for j in range(SORT_GROUP)]
        for i, j, need_max, need_min in SORT_PROGRAM:
            hi, lo = jnp.maximum(w[i], w[j]), jnp.minimum(w[i], w[j])
            if need_max:
                w[i] = hi
            if need_min:
                w[j] = lo
        for v in range(SORT_LEVELS):
            lev_scr[v, pl.ds(pl.multiple_of(kt * LR, LR), LR), :] = w[v].reshape(LR, TQ)
        return carry

    lax.fori_loop(0, nkt, tile_body, 0)

    def fill_body(kt, carry):
        keys_ref[0, 0, pl.ds(pl.multiple_of(kt * TK, TK), TK), :] = jnp.full((TK, TQ), INT_MIN, I32)
        return carry

    lax.fori_loop(nkt, keys_ref.shape[2] // TK, fill_body, 0)

    SUB = 64

    def count(pred):
        def body(kt, acc):
            r0 = pl.multiple_of(kt * TK, TK)
            for s in range(TK // SUB):
                blk = keys_ref[0, 0, pl.ds(r0 + s * SUB, SUB), :]
                acc = acc + pred(blk).astype(I32)
            return acc
        acc = lax.fori_loop(0, nkt, body, jnp.zeros((SUB, TQ), I32))
        return jnp.sum(acc, axis=0, keepdims=True)

    def count_levels(pred, levels):
        def body(kt, acc):
            r0 = pl.multiple_of(kt * LR, LR)
            for v in levels:
                acc = acc + pred(lev_scr[v, pl.ds(r0, LR), :]).astype(I32)
            return acc
        acc = lax.fori_loop(0, nkt, body, jnp.zeros((LR, TQ), I32))
        return jnp.sum(acc, axis=0, keepdims=True)

    def search(count_fn):
        def pass_body(b, carry):
            t_u, cnt_t = carry
            cand_u = t_u | lax.shift_left(jnp.int32(1), jnp.asarray(31 - b, I32))
            cand_s = cand_u ^ INT_MIN
            cnt = count_fn(lambda blk: blk >= cand_s)
            take = cnt >= topk
            return jnp.where(take, cand_u, t_u), jnp.where(take, cnt, cnt_t)
        return lax.fori_loop(0, 32, pass_body,
                             (jnp.zeros((1, TQ), I32), jnp.full((1, TQ), TAKE_ALL, I32)))

    scanned = range(SORT_LEVELS - 1)

    def search_levels():
        t_u, cnt_t = search(lambda pred: count_levels(pred, scanned))
        t_s = t_u ^ INT_MIN
        missed = count_levels(lambda blk: blk >= t_s, [SORT_LEVELS - 1])
        return t_u, cnt_t, jnp.max(missed)

    t_u, cnt_t, missed = lax.cond(
        use_levels, search_levels,
        lambda: (jnp.zeros((1, TQ), I32), jnp.full((1, TQ), TAKE_ALL, I32), jnp.int32(1)))
    full_scan = missed > 0
    t_u, cnt_t = lax.cond(full_scan, lambda: search(count), lambda: (t_u, cnt_t))
    t_s = t_u ^ INT_MIN
    ties = (cnt_t > topk) & (t_s != INT_MIN)
    any_ties = jnp.max(ties.astype(I32)) > 0
    count_gt = lambda: lax.cond(full_scan, lambda: count(lambda blk: blk > t_s),
                                lambda: count_levels(lambda blk: blk > t_s, scanned))
    cnt_gt = lax.cond(any_ties, count_gt, lambda: jnp.zeros((1, TQ), I32))
    n_take = jnp.where(ties, topk - cnt_gt, TAKE_ALL)
    row = lax.broadcasted_iota(I32, (8, TQ), 0)
    tn_ref[0, 0] = jnp.where(row == 0, t_s, n_take)


def _index_select(iq, ikb, iwT, *, TQ, TK, P, Lreal, topk):
    nB, Lp, _ = ikb.shape
    nQ = iwT.shape[1]
    return pl.pallas_call(
        functools.partial(_idx_kernel, TQ=TQ, TK=TK, P=P, Lreal=Lreal, topk=topk),
        grid=(nB, nQ),
        in_specs=[pl.BlockSpec((1, TQ, IDX_HEADS * IDX_DIM), lambda b, q: (b, q, 0)),
                  pl.BlockSpec((1, Lp, IDX_DIM), lambda b, q: (b, 0, 0)),
                  pl.BlockSpec((1, 1, IDX_HEADS, TQ), lambda b, q: (b, q, 0, 0))],
        out_specs=[pl.BlockSpec((1, 1, Lp, TQ), lambda b, q: (b, q, 0, 0)),
                   pl.BlockSpec((1, 1, 8, TQ), lambda b, q: (b, q, 0, 0))],
        out_shape=[jax.ShapeDtypeStruct((nB, nQ, Lp, TQ), I32),
                   jax.ShapeDtypeStruct((nB, nQ, 8, TQ), I32)],
        scratch_shapes=[pltpu.VMEM((SORT_LEVELS, Lp // SORT_GROUP, TQ), I32)],
        compiler_params=_params(("parallel", "arbitrary")),
        name="index_select",
    )(iq, ikb, iwT)


def _attn_kernel(qbs_ref, kts_ref, last_ref, q_ref, k_ref, vT_ref, keys_ref, tn_ref, o_ref,
                 qT_scr, ndm_scr, a_scr, m_scr, l_scr, acc_scr, tie_scr, lower_scr, *, TQ, TK, KC, P, Lreal):
    s_id = pl.program_id(1)
    qb = qbs_ref[s_id]
    kt = kts_ref[s_id]
    is_last = last_ref[s_id] == 1
    heads = [slice(h * HEAD_DIM, (h + 1) * HEAD_DIM) for h in range(N_HEADS)]

    @pl.when(kt == 0)
    def _():
        m_scr[...] = jnp.full(m_scr.shape, -jnp.inf, F32)
        l_scr[...] = jnp.zeros(l_scr.shape, F32)
        acc_scr[...] = jnp.zeros(acc_scr.shape, F32)
        tie_scr[...] = jnp.zeros(tie_scr.shape, F32)
        for sl in heads:
            qT_scr[sl, :] = q_ref[0, :, sl].astype(F32).T.astype(BF16)
        li = lax.broadcasted_iota(I32, (TK, TK), 0)
        lj = lax.broadcasted_iota(I32, (TK, TK), 1)
        lower_scr[...] = jnp.where(lj <= li, 1.0, 0.0).astype(BF16)

    keys = keys_ref[0, 0]
    thr = tn_ref[0, 0, 0:1, :]
    ntk = tn_ref[0, 0, 1:2, :]
    l = kt * TK + lax.broadcasted_iota(I32, (TK, TQ), 0)
    tq = P + qb * TQ + lax.broadcasted_iota(I32, (TK, TQ), 1)
    has_ties = jnp.max(jnp.where(ntk == TAKE_ALL, 0, 1)) > 0
    no_ties = jnp.logical_not(has_ties)

    @pl.when(jnp.logical_and(no_ties, jnp.logical_not(is_last)))
    def _():
        ndm_scr[...] = jnp.where(keys >= thr, (l - tq).astype(F32), -MASKED_DIST)

    @pl.when(jnp.logical_and(no_ties, is_last))
    def _():
        allowed = (l < Lreal) & ((l >> 6) <= (tq >> 6))
        ndm_scr[...] = jnp.where((keys >= thr) & allowed, -jnp.abs(tq - l).astype(F32), -MASKED_DIST)

    @pl.when(has_ties)
    def _():
        allowed = (l < Lreal) & ((l >> 6) <= (tq >> 6))
        eq = keys == thr
        prefix = jnp.dot(lower_scr[...], jnp.where(eq, 1.0, 0.0).astype(BF16), preferred_element_type=F32)
        rank = tie_scr[0:1, :] + prefix
        sel = ((keys > thr) | (eq & (rank <= ntk.astype(F32)))) & allowed
        ndm_scr[...] = jnp.where(sel, -jnp.abs(tq - l).astype(F32), -MASKED_DIST)
        tie_scr[0:1, :] = tie_scr[0:1, :] + prefix[TK - 1:TK, :]

    chunks = [slice(c * KC, (c + 1) * KC) for c in range(TK // KC)]

    def stage_a(h, rows, mx):
        s = jnp.dot(k_ref[0, rows, heads[h]], qT_scr[heads[h], :], preferred_element_type=F32)
        a = s + (ALIBI_SLOPES[h] * LOG2E) * ndm_scr[rows, :]
        a_scr[h % n_slots, rows, :] = a
        cm = jnp.max(a, axis=0, keepdims=True)
        return cm if mx is None else jnp.maximum(mx, cm)

    def stage_b(h, rows, m_new, pv):
        p = jnp.exp2(a_scr[h % n_slots, rows, :] - m_new).astype(BF16)
        d = jnp.dot(vT_ref[0, h, :, rows], p, preferred_element_type=F32)
        return d if pv is None else pv + d

    m_all = m_scr[...]
    l_all = l_scr[...]
    m_out, l_out = [], []
    n_slots = a_scr.shape[0]
    ahead = n_slots - 1
    mxs = [None] * (N_HEADS + ahead)
    for h0 in range(ahead):
        for rows in chunks:
            mxs[h0] = stage_a(h0, rows, mxs[h0])
    for h in range(N_HEADS):
        m_prev = m_all[h:h + 1]
        m_new = jnp.maximum(m_prev, mxs[h])
        alpha = jnp.exp2(m_prev - m_new)
        pv = None
        for rows in chunks:
            if h + ahead < N_HEADS:
                mxs[h + ahead] = stage_a(h + ahead, rows, mxs[h + ahead])
            pv = stage_b(h, rows, m_new, pv)
        acc_scr[heads[h], :] = alpha * acc_scr[heads[h], :] + pv[0:HEAD_DIM]
        l_out.append(alpha * l_all[h:h + 1] + pv[HEAD_DIM:HEAD_DIM + 1])
        m_out.append(m_new)
    m_scr[...] = jnp.concatenate(m_out, axis=0)
    l_scr[...] = jnp.concatenate(l_out, axis=0)

    @pl.when(is_last)
    def _():
        for h in range(N_HEADS):
            o_ref[0, :, heads[h]] = (acc_scr[heads[h], :] / l_scr[h:h + 1, :]).T.astype(BF16)


def _attention(aq, kb, vT, keysT, tn, *, TQ, TK, KC, P, Lreal):
    nB, Tq, _ = aq.shape
    nQ = Tq // TQ
    qbs, kts, last = [], [], []
    for qb in range(nQ):
        limit = min(Lreal, ((P + qb * TQ + TQ - 1) // CHUNK + 1) * CHUNK)
        n = -(-limit // TK)
        assert (n - 1) * TK <= P + qb * TQ
        qbs += [qb] * n
        kts += list(range(n))
        last += [0] * (n - 1) + [1]
    steps = len(qbs)
    qmap = lambda b, s, qbs, kts, last: (b, qbs[s], 0)
    grid_spec = pltpu.PrefetchScalarGridSpec(
        num_scalar_prefetch=3,
        grid=(nB, steps),
        in_specs=[pl.BlockSpec((1, TQ, WIDTH), qmap),
                  pl.BlockSpec((1, TK, WIDTH), lambda b, s, qbs, kts, last: (b, kts[s], 0)),
                  pl.BlockSpec((1, N_HEADS, VT_ROWS, TK), lambda b, s, qbs, kts, last: (b, 0, 0, kts[s])),
                  pl.BlockSpec((1, 1, TK, TQ), lambda b, s, qbs, kts, last: (b, qbs[s], kts[s], 0)),
                  pl.BlockSpec((1, 1, 8, TQ), lambda b, s, qbs, kts, last: (b, qbs[s], 0, 0))],
        out_specs=pl.BlockSpec((1, TQ, WIDTH), qmap),
        scratch_shapes=[pltpu.VMEM((WIDTH, TQ), BF16),
                        pltpu.VMEM((TK, TQ), F32),
                        pltpu.VMEM((ATTN_SCORE_SLOTS, TK, TQ), F32),
                        pltpu.VMEM((N_HEADS, TQ), F32),
                        pltpu.VMEM((N_HEADS, TQ), F32),
                        pltpu.VMEM((WIDTH, TQ), F32),
                        pltpu.VMEM((8, TQ), F32),
                        pltpu.VMEM((TK, TK), BF16)])
    arr = lambda v: jnp.asarray(np.array(v, np.int32))
    return pl.pallas_call(
        functools.partial(_attn_kernel, TQ=TQ, TK=TK, KC=KC, P=P, Lreal=Lreal),
        grid_spec=grid_spec,
        out_shape=jax.ShapeDtypeStruct((nB, Tq, WIDTH), BF16),
        compiler_params=_params(("parallel", "arbitrary")),
        name="sparse_attention",
    )(arr(qbs), arr(kts), arr(last), aq, kb, vT, keysT, tn)


def _outproj_kernel(attn_ref, ret_ref, x_ref, gate_ref, shift_ref, scale_ref, g2_ref, wo_ref, wrh_ref, wrl_ref,
                    x1_ref, h2_ref, comb_ref, *, bb, tt):
    D = x_ref.shape[-1]
    tm = bb * tt
    CW = 512
    chunks = [slice(c * CW, (c + 1) * CW) for c in range(D // CW)]

    ssq = jnp.zeros((tm, 1), F32)
    for cols in chunks:
        mix = (jnp.dot(attn_ref[...], wo_ref[0:WIDTH, cols], preferred_element_type=F32)
               + jnp.dot(ret_ref[...], wo_ref[WIDTH:2 * WIDTH, cols], preferred_element_type=F32))
        x1 = x_ref[:, cols].reshape(bb, tt, CW) + gate_ref[:, :, cols] * mix.reshape(bb, tt, CW)
        x1 = x1.reshape(tm, CW)
        x1_ref[:, cols] = x1
        ssq = ssq + jnp.sum(x1 * x1, axis=-1, keepdims=True)
    rinv = lax.rsqrt(ssq * (1.0 / D) + EPS)

    logits = jnp.zeros((tm, LANES), F32)
    for cols in chunks:
        h2 = (x1_ref[:, cols] * rinv * g2_ref[:, cols]).reshape(bb, tt, CW)
        h2 = (h2 * (1.0 + scale_ref[:, :, cols]) + shift_ref[:, :, cols]).reshape(tm, CW)
        h_hi = h2.astype(BF16)
        h2_ref[:, cols] = h_hi
        h_lo = (h2 - h_hi.astype(F32)).astype(BF16)
        logits = (logits + jnp.dot(h_hi, wrh_ref[cols, :], preferred_element_type=F32)
                  + jnp.dot(h_lo, wrh_ref[cols, :], preferred_element_type=F32)
                  + jnp.dot(h_hi, wrl_ref[cols, :], preferred_element_type=F32))
    lane = lax.broadcasted_iota(I32, logits.shape, 1)
    ninf = -jnp.inf
    rmax = lambda v: jnp.max(v, axis=-1, keepdims=True)
    rsum = lambda v: jnp.sum(v, axis=-1, keepdims=True)
    first = lambda m: jnp.min(jnp.where(m, lane, LANES), axis=-1, keepdims=True)
    gl = jnp.where(lane < N_GROUPS, logits, ninf)
    gmax = rmax(gl)
    g_top = 1.0 / rsum(jnp.exp(gl - gmax))
    g_idx = first(gl == gmax)
    emask = (lane >= N_GROUPS) & (lane < N_GROUPS + N_EXPERTS) & (((lane - N_GROUPS) >> 2) == g_idx)
    el = jnp.where(emask, logits, ninf)
    emax = rmax(el)
    esum = rsum(jnp.exp(el - emax))
    i1 = first(el == emax)
    el2 = jnp.where(lane == i1, ninf, el)
    emax2 = rmax(el2)
    i2 = first(el2 == emax2)
    p1 = 1.0 / esum
    p2 = jnp.exp(emax2 - emax) / esum
    den = p1 + p2
    comb_ref[...] = jnp.where(lane == i1, g_top * (p1 / den),
                              jnp.where(lane == i2, g_top * (p2 / den), 0.0))


def _outproj(attn, ret, x2d, gate1, shift2, scale2, g2, wo, wr_hi, wr_lo, *, bb, tt):
    N, D = x2d.shape
    tm = bb * tt
    tok = lambda w: pl.BlockSpec((tm, w), lambda i: (i, 0))
    mod = pl.BlockSpec((bb, 1, D), lambda i: (i if bb > 1 else 0, 0, 0))
    full = lambda a: pl.BlockSpec(a.shape, lambda i: (0,) * a.ndim)
    return pl.pallas_call(
        functools.partial(_outproj_kernel, bb=bb, tt=tt),
        grid=(N // tm,),
        in_specs=[tok(WIDTH), tok(WIDTH), tok(D), mod, mod, mod, full(g2), full(wo), full(wr_hi), full(wr_lo)],
        out_specs=[tok(D), tok(D), tok(LANES)],
        out_shape=[jax.ShapeDtypeStruct((N, D), F32), jax.ShapeDtypeStruct((N, D), BF16),
                   jax.ShapeDtypeStruct((N, LANES), F32)],
        compiler_params=_params(("parallel",)),
        name="outproj_router",
    )(attn, ret, x2d, gate1, shift2, scale2, g2, wo, wr_hi, wr_lo)


def _moe_kernel(h_ref, comb_ref, x1_ref, gate_ref, wg_ref, wu_ref, wd_ref, y_ref, acc_scr, *, bb, tt):
    e = pl.program_id(1)
    D = x1_ref.shape[-1]

    @pl.when(e == 0)
    def _():
        acc_scr[...] = jnp.zeros(acc_scr.shape, F32)

    h = h_ref[...]
    a = jnp.dot(h, wg_ref[0], preferred_element_type=F32)
    b = jnp.dot(h, wu_ref[0], preferred_element_type=F32)
    mid = (_silu(a) * b).astype(BF16)
    out = jnp.dot(mid, wd_ref[0], preferred_element_type=F32)
    lane = lax.broadcasted_iota(I32, comb_ref.shape, 1)
    w = jnp.sum(jnp.where(lane == e + N_GROUPS, comb_ref[...], 0.0), axis=-1, keepdims=True)
    acc_scr[...] += w * out

    @pl.when(e == N_EXPERTS - 1)
    def _():
        y = x1_ref[...].reshape(bb, tt, D) + gate_ref[...] * acc_scr[...].reshape(bb, tt, D)
        y_ref[...] = y.reshape(bb * tt, D)


def _moe(h2, comb, x1, gate2, wg, wu, wd, *, bb, tt):
    N, D = x1.shape
    tm = bb * tt
    tok = lambda w: pl.BlockSpec((tm, w), lambda i, e: (i, 0))
    mod = pl.BlockSpec((bb, 1, D), lambda i, e: (i if bb > 1 else 0, 0, 0))
    return pl.pallas_call(
        functools.partial(_moe_kernel, bb=bb, tt=tt),
        grid=(N // tm, N_EXPERTS),
        in_specs=[tok(D), tok(LANES), tok(D), mod,
                  pl.BlockSpec((1, D, D_EXPERT), lambda i, e: (e, 0, 0)),
                  pl.BlockSpec((1, D, D_EXPERT), lambda i, e: (e, 0, 0)),
                  pl.BlockSpec((1, D_EXPERT, D), lambda i, e: (e, 0, 0))],
        out_specs=tok(D),
        out_shape=jax.ShapeDtypeStruct((N, D), F32),
        scratch_shapes=[pltpu.VMEM((tm, D), F32)],
        compiler_params=_params(("parallel", "arbitrary")),
        name="moe",
    )(h2, comb, x1, gate2, wg, wu, wd)


def _moe_routed_kernel(h_ref, comb_ref, x1_hbm, gate_ref, wg_ref, wu_ref, wd_ref, y_ref,
                       rank_scr, rankT_scr, gT_scr, sem):
    i = pl.program_id(0)
    e = pl.program_id(1)
    tm, D = y_ref.shape
    R, KP = MOE_ROWS, MOE_SCATTER_K

    @pl.when(e == 0)
    def _():
        residual = pltpu.make_async_copy(x1_hbm.at[pl.ds(i * tm, tm), :], y_ref, sem)
        residual.start()
        comb = comb_ref[...]
        used = comb != 0.0
        ti = lax.broadcasted_iota(I32, (tm, tm), 0)
        tj = lax.broadcasted_iota(I32, (tm, tm), 1)
        earlier = jnp.where(tj < ti, 1.0, 0.0).astype(BF16)
        rank = jnp.dot(earlier, jnp.where(used, 1.0, 0.0).astype(BF16), preferred_element_type=F32)
        rank = jnp.where(used, rank, -1.0)
        rank_scr[...] = rank
        rankT_scr[...] = rank.T
        gT_scr[...] = comb.T
        residual.wait()

    lane_e = e + N_GROUPS
    lane = lax.broadcasted_iota(I32, (tm, LANES), 1)
    r_col = jnp.sum(jnp.where(lane == lane_e, rank_scr[...], 0.0), axis=-1, keepdims=True)
    r_row = rankT_scr[pl.ds(lane_e, 1), :]
    g_row = gT_scr[pl.ds(lane_e, 1), :]
    n_e = jnp.sum(jnp.where(r_row >= 0.0, 1, 0))
    gate2 = gate_ref[0]
    CW = 512

    def chunk(c, carry):
        base = jnp.asarray(c * R, F32)
        ridx = base + lax.broadcasted_iota(I32, (R, tm), 0).astype(F32)
        pm = r_row == ridx
        x = jnp.dot(jnp.where(pm, 1.0, 0.0).astype(BF16), h_ref[...], preferred_element_type=F32).astype(BF16)
        a = jnp.dot(x, wg_ref[0], preferred_element_type=F32)
        b = jnp.dot(x, wu_ref[0], preferred_element_type=F32)
        mid = (_silu(a) * b).astype(BF16)
        y = jnp.dot(mid, wd_ref[0], preferred_element_type=F32)
        g_r = jnp.sum(jnp.where(pm, g_row, 0.0), axis=-1, keepdims=True)
        ys = (y * g_r * gate2).astype(BF16)
        ys = jnp.concatenate([ys, jnp.zeros((KP - R, D), BF16)], axis=0)
        cidx = lax.broadcasted_iota(I32, (tm, KP), 1)
        sm = (r_col == base + cidx.astype(F32)) & (cidx < R)
        s = jnp.where(sm, 1.0, 0.0).astype(BF16)
        for c0 in range(0, D, CW):
            y_ref[:, c0:c0 + CW] += jnp.dot(s, ys[:, c0:c0 + CW], preferred_element_type=F32)
        return carry

    lax.fori_loop(0, (n_e + R - 1) // R, chunk, 0)


def _moe_routed(h2, comb, x1, gate2, wg, wu, wd):
    N, D = x1.shape
    tm = MOE_TOKENS
    assert gate2.shape[0] == 1 and N % tm == 0
    tok = lambda w: pl.BlockSpec((tm, w), lambda i, e: (i, 0))
    return pl.pallas_call(
        _moe_routed_kernel,
        grid=(N // tm, N_EXPERTS),
        in_specs=[tok(D), tok(LANES), pl.BlockSpec(memory_space=pl.ANY),
                  pl.BlockSpec((1, 1, D), lambda i, e: (0, 0, 0)),
                  pl.BlockSpec((1, D, D_EXPERT), lambda i, e: (e, 0, 0)),
                  pl.BlockSpec((1, D, D_EXPERT), lambda i, e: (e, 0, 0)),
                  pl.BlockSpec((1, D_EXPERT, D), lambda i, e: (e, 0, 0))],
        out_specs=tok(D),
        out_shape=jax.ShapeDtypeStruct((N, D), F32),
        scratch_shapes=[pltpu.VMEM((tm, LANES), F32), pltpu.VMEM((LANES, tm), F32),
                        pltpu.VMEM((LANES, tm), F32), pltpu.SemaphoreType.DMA(())],
        compiler_params=_params(("parallel", "arbitrary")),
        name="moe_routed",
    )(h2, comb, x1, gate2, wg, wu, wd)


def _layer(x, mod, past, W, *, bb, tt, TQ, TK, KC, ret_chunk):
    B, T, D = x.shape
    N = B * T
    x2d = x.reshape(N, D)
    shift1, scale1, gate1, shift2, scale2, gate2 = [m.reshape(B, 1, D) for m in jnp.split(mod, 6, axis=-1)]
    z = _inproj(x2d, shift1, scale1, W["g1"], W["w_in"], W["qg"], W["kg"], bb=bb, tt=tt)
    vT_new = z["avT"].reshape(N_HEADS, VT_ROWS, B, T)
    vT_new = vT_new.reshape(1, N_HEADS, VT_ROWS, T) if B == 1 else vT_new.transpose(2, 0, 1, 3)

    if past is None:
        state0 = jnp.zeros((B, N_HEADS, HEAD_DIM, HEAD_DIM), F32)
        P, Lreal = 0, T
        topk = min(TOPK_MAX, T // 4)
        kb = z["akb"].reshape(B, T, WIDTH)
        vT = vT_new
        ikb = z["ikb"].reshape(B, T, IDX_DIM)
        Tq = T
        pad_q = lambda a: a.reshape(B, T, a.shape[-1])
    else:
        ck, cv, cki, state0 = past
        P = ck.shape[1]
        Lreal = P + T
        topk = min(TOPK_MAX, Lreal // 4)
        Lp = -(-Lreal // TK) * TK
        kb, vT, ikb = _assemble_cache(ck, cv, cki, z["akb"].reshape(B, T, WIDTH), vT_new,
                                      z["ikb"].reshape(B, T, IDX_DIM), Lp)
        Tq = -(-T // TQ) * TQ
        pad_q = lambda a: jnp.pad(a.reshape(B, T, a.shape[-1]), ((0, 0), (0, Tq - T), (0, 0)))

    iq = pad_q(z["iq"])
    iwT = pad_q(z["iw"]).reshape(B, Tq // TQ, TQ, IDX_HEADS).transpose(0, 1, 3, 2)
    keysT, tn = _index_select(iq, ikb, iwT, TQ=TQ, TK=TK, P=P, Lreal=Lreal, topk=topk)
    attn = _attention(pad_q(z["aq"]), kb, vT, keysT, tn, TQ=TQ, TK=TK, KC=KC, P=P, Lreal=Lreal)
    attn = attn[:, :T].reshape(N, WIDTH)

    ret, ret_state = _retention(z["rq"], z["rk"], z["rv"], z["rg"], state0, B=B, T=T, C=ret_chunk)
    x1, h2, comb = _outproj(attn, ret, x2d, gate1, shift2, scale2, W["g2"], W["w_out"], W["w_r_hi"],
                            W["w_r_lo"], bb=bb, tt=tt)
    if B == 1 and N % MOE_TOKENS == 0:
        y = _moe_routed(h2, comb, x1, gate2, W["wg"], W["wu"], W["wd"])
    else:
        y = _moe(h2, comb, x1, gate2, W["wg"], W["wu"], W["wd"], bb=bb, tt=tt)
    return (y.reshape(B, T, D), z["ak"].reshape(B, T, N_HEADS, HEAD_DIM),
            z["av"].reshape(B, T, N_HEADS, HEAD_DIM), z["ik"].reshape(B, T, IDX_DIM), ret_state)


def kernel(x_prompt, x_sample, cache_k, cache_v, cache_kidx, state_ret, c_prompt, c_sample, w_ada, b_ada,
           norm1_g, norm2_g, w_in, q_norm_g, k_norm_g, w_out, w_group, w_router, w_gate_e, w_up_e, w_down_e):
    depth = w_ada.shape[0]
    D = D_MODEL
    Bp, Bs = x_prompt.shape[0], x_sample.shape[0]
    Ts = x_sample.shape[1]
    xp, xs = x_prompt, x_sample
    outs_p, outs_s = [], []
    for l in range(depth):
        w_r = jnp.concatenate(
            [w_group[l], jnp.moveaxis(w_router[l], 0, 1).reshape(D, N_EXPERTS),
             jnp.zeros((D, LANES - N_GROUPS - N_EXPERTS), F32)], axis=1)
        w_r_hi = w_r.astype(BF16)
        W = {
            "g1": norm1_g[l].reshape(1, D), "g2": norm2_g[l].reshape(1, D),
            "qg": q_norm_g[l].reshape(1, HEAD_DIM), "kg": k_norm_g[l].reshape(1, HEAD_DIM),
            "w_in": _pad_cast_w_in(w_in[l]),
            "w_out": w_out[l].astype(BF16),
            "w_r_hi": w_r_hi, "w_r_lo": (w_r - w_r_hi.astype(F32)).astype(BF16),
            "wg": w_gate_e[l].astype(BF16), "wu": w_up_e[l].astype(BF16), "wd": w_down_e[l].astype(BF16),
        }
        rows = Bp + Bs
        rows_p = -(-rows // 8) * 8
        c_all = jnp.concatenate([c_prompt, c_sample, jnp.zeros((rows_p - rows, D), F32)], axis=0)
        mod = _adaln_mod(c_all, w_ada[l], b_ada[l])
        past = (cache_k[l], cache_v[l], cache_kidx[l], state_ret[l])
        xp, kp, vp, kip, sp = _layer(xp, mod[:Bp], None, W, bb=1, tt=512, TQ=256, TK=512, KC=256,
                                     ret_chunk=256)
        xs, kn, vn, kin, sn = _layer(xs, mod[Bp:rows], past, W, bb=512 // Ts, tt=Ts, TQ=128, TK=384, KC=384,
                                     ret_chunk=Ts)
        outs_p.append((kp, vp, kip, sp))
        outs_s.append((kn, vn, kin, sn))
    st = lambda xs_, i: jnp.stack([o[i] for o in xs_])
    return (xp, xs, st(outs_p, 0), st(outs_p, 1), st(outs_p, 2), st(outs_p, 3),
            st(outs_s, 0), st(outs_s, 1), st(outs_s, 2), st(outs_s, 3))
```
